```python
import math
import numpy as np
import jax
import jax.numpy as jnp
from jax import lax

D_MODEL = 1024
BATCH = 16
SEQ = 2048
DEPTH = 2
DEC_BATCH = 128
DEC_SEQ = 4
PAST_LEN = 8192
PAGE_SIZE = 128

HEAD_DIM = 64
MOBA_HEADS = 6
MOBA_KV_HEADS = 2
MOBA_GROUP = MOBA_HEADS // MOBA_KV_HEADS
MOBA_BLOCK = 256
MOBA_TOPK = 3
MOBA_QBLK = 64
DIFF_HEADS = 4
DIFF_QK_DIM = 32
DIFF_V_DIM = 2 * DIFF_QK_DIM
MLA_HEADS = 6
MLA_Q_LORA = 256
MLA_KV_LORA = 128
MLA_NOPE_DIM = 64
MLA_ROPE_DIM = 32
MLA_V_DIM = 64
D_FF = 4 * D_MODEL
ATTN_QBLK = 128
ROPE_THETA = 10000.0
NORM_EPS = 1e-6
SUBLN_EPS = 1e-5
NEG_INF = -1e30

IN_SIZES = (MOBA_HEADS * HEAD_DIM, MOBA_KV_HEADS * HEAD_DIM, MOBA_KV_HEADS * HEAD_DIM,
            DIFF_HEADS * 2 * DIFF_QK_DIM, DIFF_HEADS * 2 * DIFF_QK_DIM, DIFF_HEADS * DIFF_V_DIM,
            MLA_Q_LORA, MLA_KV_LORA + MLA_ROPE_DIM)
D_IN = sum(IN_SIZES)
D_MIX = MOBA_HEADS * HEAD_DIM + DIFF_HEADS * DIFF_V_DIM + MLA_HEADS * MLA_V_DIM

kernel_name = 'hybrid_moba_diff_mla_decode_step'


def rms_norm(x, g, eps=NORM_EPS):
    xf = x.astype(jnp.float32)
    y = xf * lax.rsqrt(jnp.mean(xf * xf, axis=-1, keepdims=True) + eps)
    return (y * g.astype(jnp.float32)).astype(x.dtype)


def rope(x, pos):
    d = x.shape[-1]
    half = d // 2
    inv = 1.0 / (ROPE_THETA ** (jnp.arange(half, dtype=jnp.float32) * (2.0 / d)))
    ang = pos.astype(jnp.float32)[:, None] * inv[None, :]
    shape = (pos.shape[0],) + (1,) * (x.ndim - 3) + (half,)
    cos = jnp.cos(ang).reshape(shape)
    sin = jnp.sin(ang).reshape(shape)
    xf = x.astype(jnp.float32)
    x1, x2 = xf[..., :half], xf[..., half:]
    return jnp.concatenate([x1 * cos - x2 * sin, x2 * cos + x1 * sin], axis=-1).astype(x.dtype)


def gather_pages(pool, page_table):
    g = pool[page_table]
    return g.reshape((page_table.shape[0], page_table.shape[1] * PAGE_SIZE) + pool.shape[2:])


def sweep_query_blocks(core, queries, qblk):
    n_q = queries[0].shape[1]

    def blk(i):
        q0 = i * qblk
        sl = [lax.dynamic_slice_in_dim(q, q0, qblk, axis=1) for q in queries]
        return core(sl, q0 + jnp.arange(qblk))

    o = lax.map(blk, jnp.arange(n_q // qblk))
    o = jnp.moveaxis(o, 0, 1)
    return o.reshape((o.shape[0], n_q) + o.shape[3:])


def mixer_inputs(h, pos, w_in, g_mla_q, w_mla_uq, g_mla_kv, w_mla_uk):
    B, S, _ = h.shape
    z = h @ w_in
    offs = np.cumsum(IN_SIZES)[:-1].tolist()
    mq, mk, mv, dq, dk, dv, cq, ckv_kpe = jnp.split(z, offs, axis=-1)
    mq = rope(mq.reshape(B, S, MOBA_HEADS, HEAD_DIM), pos)
    mk = rope(mk.reshape(B, S, MOBA_KV_HEADS, HEAD_DIM), pos)
    mv = mv.reshape(B, S, MOBA_KV_HEADS, HEAD_DIM)
    dq = rope(dq.reshape(B, S, DIFF_HEADS, 2, DIFF_QK_DIM), pos)
    dk = rope(dk.reshape(B, S, DIFF_HEADS, 2, DIFF_QK_DIM), pos)
    dv = dv.reshape(B, S, DIFF_HEADS, DIFF_V_DIM)
    q = (rms_norm(cq, g_mla_q) @ w_mla_uq).reshape(B, S, MLA_HEADS, MLA_NOPE_DIM + MLA_ROPE_DIM)
    q_lat = jnp.einsum('bshn,chn->bshc', q[..., :MLA_NOPE_DIM], w_mla_uk)
    q_pe = rope(q[..., MLA_NOPE_DIM:], pos)
    ckv = rms_norm(ckv_kpe[..., :MLA_KV_LORA], g_mla_kv)
    kpe = rope(ckv_kpe[..., MLA_KV_LORA:], pos)
    return mq, mk, mv, dq, dk, dv, q_lat, q_pe, ckv, kpe


def moba_attend(qg, ko, vo, own_ok, ksel=None, vsel=None, sel_ok=None):
    B, T, G, R, D = qg.shape
    scale = D ** -0.5
    s_own = jnp.einsum('btgrd,blgd->bgrtl', qg, ko).astype(jnp.float32) * scale
    s_own = jnp.where(own_ok, s_own, NEG_INF)
    if ksel is None:
        p = jax.nn.softmax(s_own, axis=-1).astype(vo.dtype)
        o = jnp.einsum('bgrtl,blgd->btgrd', p, vo)
    else:
        s_sel = jnp.einsum('btgrd,bgrtkld->bgrtkl', qg, ksel).astype(jnp.float32) * scale
        s_sel = jnp.where(sel_ok[..., None], s_sel, NEG_INF).reshape(B, G, R, T, -1)
        m = s_sel.shape[-1]
        p = jax.nn.softmax(jnp.concatenate([s_sel, s_own], axis=-1), axis=-1).astype(vo.dtype)
        o = (jnp.einsum('bgrtm,bgrtmd->btgrd', p[..., :m], vsel.reshape(B, G, R, T, m, D))
             + jnp.einsum('bgrtl,blgd->btgrd', p[..., m:], vo))
    return o.reshape(B, T, G * R * D)


def moba_prompt(q, k, v):
    B, S, H, D = q.shape
    G, R = MOBA_KV_HEADS, MOBA_GROUP
    nb = -(-S // MOBA_BLOCK)
    pad = nb * MOBA_BLOCK - S
    kp = jnp.pad(k, ((0, 0), (0, pad), (0, 0), (0, 0)))
    vp = jnp.pad(v, ((0, 0), (0, pad), (0, 0), (0, 0)))
    kb = kp.reshape(B, nb, MOBA_BLOCK, G, D).transpose(0, 3, 1, 2, 4)
    vb = vp.reshape(B, nb, MOBA_BLOCK, G, D).transpose(0, 3, 1, 2, 4)
    kmean = jnp.mean(kb.astype(jnp.float32), axis=3)
    n_sel = min(MOBA_TOPK, nb - 1)
    bidx = jnp.arange(B)[:, None, None, None, None]
    gidx = jnp.arange(G)[None, :, None, None, None]
    blk_pos = jnp.arange(MOBA_BLOCK)

    def core(sl, qpos):
        qc = sl[0]
        qg = qc.reshape(B, qc.shape[1], G, R, D)
        cur = qpos[0] // MOBA_BLOCK
        start = cur * MOBA_BLOCK
        ko = lax.dynamic_slice_in_dim(kp, start, MOBA_BLOCK, axis=1)
        vo = lax.dynamic_slice_in_dim(vp, start, MOBA_BLOCK, axis=1)
        own_ok = (start + blk_pos)[None, :] <= qpos[:, None]
        if n_sel == 0:
            return moba_attend(qg, ko, vo, own_ok)
        gate = jnp.einsum('btgrd,bgnd->bgrtn', qg.astype(jnp.float32), kmean)
        gate = jnp.where(jnp.arange(nb) < cur, gate, -jnp.inf)
        _, sel = lax.top_k(gate, n_sel)
        sel_ok = sel < cur
        return moba_attend(qg, ko, vo, own_ok, kb[bidx, gidx, sel], vb[bidx, gidx, sel], sel_ok)

    return sweep_query_blocks(core, [q], MOBA_QBLK)


def moba_sample(q, k_new, v_new, pool_k, pool_v, page_table):
    Bd, T, H, D = q.shape
    G, R = MOBA_KV_HEADS, MOBA_GROUP
    past_len = page_table.shape[1] * PAGE_SIZE
    n_full = past_len // MOBA_BLOCK
    own_start = n_full * MOBA_BLOCK
    ppb = MOBA_BLOCK // PAGE_SIZE
    qg = q.reshape(Bd, T, G, R, D)
    qpos = past_len + jnp.arange(T)
    k_past = gather_pages(pool_k, page_table)
    v_own = gather_pages(pool_v, page_table[:, own_start // PAGE_SIZE:])
    ko = jnp.concatenate([k_past[:, own_start:], k_new], axis=1)
    vo = jnp.concatenate([v_own, v_new], axis=1)
    opos = jnp.concatenate([jnp.arange(own_start, past_len), qpos])
    own_ok = opos[None, :] <= qpos[:, None]
    n_sel = min(MOBA_TOPK, n_full)
    if n_sel == 0:
        return moba_attend(qg, ko, vo, own_ok)
    kb = k_past[:, :own_start].reshape(Bd, n_full, MOBA_BLOCK, G, D).transpose(0, 3, 1, 2, 4)
    kmean = jnp.mean(kb.astype(jnp.float32), axis=3)
    gate = jnp.einsum('btgrd,bgnd->bgrtn', qg.astype(jnp.float32), kmean)
    _, sel = lax.top_k(gate, n_sel)
    bidx = jnp.arange(Bd)[:, None, None, None, None]
    gidx = jnp.arange(G)[None, :, None, None, None]
    ksel = kb[bidx, gidx, sel]
    pages = page_table[bidx[..., None], sel[..., None] * ppb + jnp.arange(ppb)]
    vsel = pool_v[pages, :, gidx[..., None]].reshape(Bd, G, R, T, n_sel, MOBA_BLOCK, D)
    sel_ok = jnp.ones(sel.shape, dtype=bool)
    return moba_attend(qg, ko, vo, own_ok, ksel, vsel, sel_ok)


def diff_core(q, k, v, qpos, kpos, lam, lam_init, g_sub):
    B, T = q.shape[:2]
    scale = DIFF_QK_DIM ** -0.5
    s = jnp.einsum('bqhcd,bkhcd->bhcqk', q, k).astype(jnp.float32) * scale
    s = jnp.where(kpos[None, :] <= qpos[:, None], s, NEG_INF)
    p = jax.nn.softmax(s, axis=-1)
    a = (p[:, :, 0] - lam * p[:, :, 1]).astype(v.dtype)
    o = jnp.einsum('bhqk,bkhv->bqhv', a, v)
    o = rms_norm(o, g_sub, SUBLN_EPS) * (1.0 - lam_init)
    return o.reshape(B, T, DIFF_HEADS * DIFF_V_DIM)


def mla_core(q_lat, q_pe, ckv, kpe, qpos, kpos):
    scale = (MLA_NOPE_DIM + MLA_ROPE_DIM) ** -0.5
    s = (jnp.einsum('bqhc,bkc->bhqk', q_lat, ckv)
         + jnp.einsum('bqhr,bkr->bhqk', q_pe, kpe)).astype(jnp.float32) * scale
    s = jnp.where(kpos[None, :] <= qpos[:, None], s, NEG_INF)
    p = jax.nn.softmax(s, axis=-1).astype(ckv.dtype)
    return jnp.einsum('bhqk,bkc->bqhc', p, ckv)


def mix_out(o_moba, o_diff, o_lat, w_uv, w_o):
    B, T = o_moba.shape[:2]
    o_mla = jnp.einsum('bthc,chv->bthv', o_lat, w_uv).reshape(B, T, MLA_HEADS * MLA_V_DIM)
    return jnp.concatenate([o_moba, o_diff, o_mla], axis=-1) @ w_o


def mlp(h, w_up, w_down):
    return jnp.square(jax.nn.relu(h @ w_up)) @ w_down


def setup_inputs(seed: int = 0) -> dict:
    key = jax.random.key(seed)
    ks = jax.random.split(key, 28)
    n_pages = PAST_LEN // PAGE_SIZE
    n_used = DEC_BATCH * n_pages
    n_pool = n_used + n_used // 4

    def nrm(k, shape, scale=1.0):
        return scale * jax.random.normal(k, shape, jnp.float32)

    def gain(k, shape):
        return 1.0 + 0.05 * jax.random.normal(k, shape, jnp.float32)

    page_table = jax.random.permutation(ks[8], n_pool)[:n_used].reshape(DEC_BATCH, n_pages).astype(jnp.int32)
    return {
        'x_prompt': nrm(ks[0], (BATCH, SEQ, D_MODEL)),
        'x_sample': nrm(ks[1], (DEC_BATCH, DEC_SEQ, D_MODEL)),
        'cache_moba_k': nrm(ks[2], (DEPTH, n_pool, PAGE_SIZE, MOBA_KV_HEADS, HEAD_DIM)),
        'cache_moba_v': nrm(ks[3], (DEPTH, n_pool, PAGE_SIZE, MOBA_KV_HEADS, HEAD_DIM)),
        'cache_diff_k': nrm(ks[4], (DEPTH, n_pool, PAGE_SIZE, DIFF_HEADS, 2, DIFF_QK_DIM)),
        'cache_diff_v': nrm(ks[5], (DEPTH, n_pool, PAGE_SIZE, DIFF_HEADS, DIFF_V_DIM)),
        'cache_mla_ckv': nrm(ks[6], (DEPTH, n_pool, PAGE_SIZE, MLA_KV_LORA)),
        'cache_mla_kpe': nrm(ks[7], (DEPTH, n_pool, PAGE_SIZE, MLA_ROPE_DIM)),
        'page_table': page_table,
        'g_attn': gain(ks[9], (DEPTH, D_MODEL)),
        'w_in': nrm(ks[10], (DEPTH, D_MODEL, D_IN), D_MODEL ** -0.5),
        'g_mla_q': gain(ks[11], (DEPTH, MLA_Q_LORA)),
        'w_mla_uq': nrm(ks[12], (DEPTH, MLA_Q_LORA, MLA_HEADS * (MLA_NOPE_DIM + MLA_ROPE_DIM)), MLA_Q_LORA ** -0.5),
        'g_mla_kv': gain(ks[13], (DEPTH, MLA_KV_LORA)),
        'w_mla_uk': nrm(ks[14], (DEPTH, MLA_KV_LORA, MLA_HEADS, MLA_NOPE_DIM), MLA_KV_LORA ** -0.5),
        'w_mla_uv': nrm(ks[15], (DEPTH, MLA_KV_LORA, MLA_HEADS, MLA_V_DIM), MLA_KV_LORA ** -0.5),
        'lambda_q1': nrm(ks[16], (DEPTH, DIFF_QK_DIM), 0.1),
        'lambda_k1': nrm(ks[17], (DEPTH, DIFF_QK_DIM), 0.1),
        'lambda_q2': nrm(ks[18], (DEPTH, DIFF_QK_DIM), 0.1),
        'lambda_k2': nrm(ks[19], (DEPTH, DIFF_QK_DIM), 0.1),
        'g_diff_sub': gain(ks[20], (DEPTH, DIFF_V_DIM)),
        'w_out': nrm(ks[21], (DEPTH, D_MIX, D_MODEL), D_MIX ** -0.5),
        'g_mlp': gain(ks[22], (DEPTH, D_MODEL)),
        'w_up': nrm(ks[23], (DEPTH, D_MODEL, D_FF), D_MODEL ** -0.5),
        'w_down': nrm(ks[24], (DEPTH, D_FF, D_MODEL), D_FF ** -0.5),
        'g_final': gain(ks[25], (D_MODEL,)),
    }


def reference(x_prompt, x_sample, cache_moba_k, cache_moba_v, cache_diff_k, cache_diff_v,
              cache_mla_ckv, cache_mla_kpe, page_table, g_attn, w_in, g_mla_q, w_mla_uq,
              g_mla_kv, w_mla_uk, w_mla_uv, lambda_q1, lambda_k1, lambda_q2, lambda_k2,
              g_diff_sub, w_out, g_mlp, w_up, w_down, g_final):
    S = x_prompt.shape[1]
    T = x_sample.shape[1]
    past_len = page_table.shape[1] * PAGE_SIZE
    pos_p = jnp.arange(S)
    pos_s = past_len + jnp.arange(T)
    kpos_s = jnp.arange(past_len + T)
    xp, xs = x_prompt, x_sample
    new_p = [[] for _ in range(6)]
    new_s = [[] for _ in range(6)]
    for l in range(DEPTH):
        lam_init = 0.8 - 0.6 * math.exp(-0.3 * l)
        lam = (jnp.exp(jnp.sum(lambda_q1[l].astype(jnp.float32) * lambda_k1[l].astype(jnp.float32)))
               - jnp.exp(jnp.sum(lambda_q2[l].astype(jnp.float32) * lambda_k2[l].astype(jnp.float32)))
               + lam_init)
        mix_w = (w_in[l], g_mla_q[l], w_mla_uq[l], g_mla_kv[l], w_mla_uk[l])

        hp = rms_norm(xp, g_attn[l])
        mq, mk, mv, dq, dk, dv, ql, qpe, ckv, kpe = mixer_inputs(hp, pos_p, *mix_w)
        o_moba = moba_prompt(mq, mk, mv)
        o_diff = sweep_query_blocks(
            lambda sl, qpos: diff_core(sl[0], dk, dv, qpos, pos_p, lam, lam_init, g_diff_sub[l]),
            [dq], ATTN_QBLK)
        o_lat = sweep_query_blocks(
            lambda sl, qpos: mla_core(sl[0], sl[1], ckv, kpe, qpos, pos_p),
            [ql, qpe], ATTN_QBLK)
        xp = xp + mix_out(o_moba, o_diff, o_lat, w_mla_uv[l], w_out[l])
        xp = xp + mlp(rms_norm(xp, g_mlp[l]), w_up[l], w_down[l])
        for lst, a in zip(new_p, (mk, mv, dk, dv, ckv, kpe)):
            lst.append(a)

        hs = rms_norm(xs, g_attn[l])
        mq_s, mk_s, mv_s, dq_s, dk_s, dv_s, ql_s, qpe_s, ckv_s, kpe_s = mixer_inputs(hs, pos_s, *mix_w)
        o_moba_s = moba_sample(mq_s, mk_s, mv_s, cache_moba_k[l], cache_moba_v[l], page_table)
        kd = jnp.concatenate([gather_pages(cache_diff_k[l], page_table), dk_s], axis=1)
        vd = jnp.concatenate([gather_pages(cache_diff_v[l], page_table), dv_s], axis=1)
        o_diff_s = diff_core(dq_s, kd, vd, pos_s, kpos_s, lam, lam_init, g_diff_sub[l])
        ckv_all = jnp.concatenate([gather_pages(cache_mla_ckv[l], page_table), ckv_s], axis=1)
        kpe_all = jnp.concatenate([gather_pages(cache_mla_kpe[l], page_table), kpe_s], axis=1)
        o_lat_s = mla_core(ql_s, qpe_s, ckv_all, kpe_all, pos_s, kpos_s)
        xs = xs + mix_out(o_moba_s, o_diff_s, o_lat_s, w_mla_uv[l], w_out[l])
        xs = xs + mlp(rms_norm(xs, g_mlp[l]), w_up[l], w_down[l])
        for lst, a in zip(new_s, (mk_s, mv_s, dk_s, dv_s, ckv_s, kpe_s)):
            lst.append(a)

    y_prompt = rms_norm(xp, g_final)
    y_sample = rms_norm(xs, g_final)
    mk_p, mv_p, dk_p, dv_p, ckv_p, kpe_p = [jnp.stack(lst, axis=0) for lst in new_p]
    mk_n, mv_n, dk_n, dv_n, ckv_n, kpe_n = [jnp.stack(lst, axis=0) for lst in new_s]
    return (y_prompt, y_sample, mk_p, mv_p, dk_p, dv_p, ckv_p, kpe_p,
            mk_n, mv_n, dk_n, dv_n, ckv_n, kpe_n)
```

```python
import functools
import math

import jax
import jax.numpy as jnp
from jax import lax
from jax.experimental import pallas as pl
from jax.experimental.pallas import tpu as pltpu

F32 = jnp.float32
BF16 = jnp.bfloat16

HEAD_DIM = 64
MOBA_HEADS = 6
MOBA_KV_HEADS = 2
MOBA_GROUP = MOBA_HEADS // MOBA_KV_HEADS
MOBA_BLOCK = 256
MOBA_TOPK = 3
DIFF_HEADS = 4
DIFF_QK_DIM = 32
DIFF_V_DIM = 2 * DIFF_QK_DIM
MLA_HEADS = 6
MLA_Q_LORA = 256
MLA_KV_LORA = 128
MLA_NOPE_DIM = 64
MLA_ROPE_DIM = 32
MLA_V_DIM = 64
ROPE_THETA = 10000.0
NORM_EPS = 1e-6
SUBLN_EPS = 1e-5
NEG_INF = -1e30

LANES = 128
ATTN_BLK = 256
PROJ_ROWS = 512
FF_CHUNK = 1024
PAGE_UNROLL = 8
VMEM_LIMIT = 48 * 1024 * 1024

_IN_SIZES = (MOBA_HEADS * HEAD_DIM, MOBA_KV_HEADS * HEAD_DIM, MOBA_KV_HEADS * HEAD_DIM,
             DIFF_HEADS * 2 * DIFF_QK_DIM, DIFF_HEADS * 2 * DIFF_QK_DIM, DIFF_HEADS * DIFF_V_DIM,
             MLA_Q_LORA, MLA_KV_LORA, MLA_ROPE_DIM)
_OFF = [0]
for _s in _IN_SIZES:
    _OFF.append(_OFF[-1] + _s)
(_O_MQ, _O_MK, _O_MV, _O_DQ, _O_DK, _O_DV, _O_CQ, _O_CKV, _O_KPE, _O_END) = _OFF

D_MQ = MOBA_HEADS * HEAD_DIM
D_MKV = MOBA_KV_HEADS * HEAD_DIM
D_DQK = DIFF_HEADS * 2 * DIFF_QK_DIM
D_DV = DIFF_HEADS * DIFF_V_DIM
D_QM = MLA_HEADS * 2 * LANES
D_LAT = MLA_HEADS * MLA_KV_LORA

_R_MQ, _R_DQ, _R_CQ, _R_CKV, _R_KPE = 0, D_MQ, D_MQ + D_DQK, D_MQ + D_DQK + MLA_Q_LORA, D_MQ + D_DQK + MLA_Q_LORA + MLA_KV_LORA
R_COLS = _R_KPE + LANES
_T_MK, _T_MV, _T_DK, _T_DV, _T_KPE = 0, D_MKV, 2 * D_MKV, 2 * D_MKV + D_DQK, 2 * D_MKV + D_DQK + D_DV
T_ROWS = _T_KPE + MLA_ROPE_DIM


def _rms(x, g, eps):
    return x * lax.rsqrt(jnp.mean(x * x, axis=-1, keepdims=True) + eps) * g


def _nt(a, b):
    return lax.dot_general(a, b, (((1,), (1,)), ((), ())), preferred_element_type=F32)


def _mm(a, b):
    return jnp.dot(a, b, preferred_element_type=F32)


def _rope_rows(x, cos, sin_signed, half):
    lane = lax.broadcasted_iota(jnp.int32, x.shape, 1)
    first = (lane & (2 * half - 1)) < half
    rot = jnp.where(first, pltpu.roll(x, LANES - half, axis=1), pltpu.roll(x, half, axis=1))
    return x * cos + rot * sin_signed


def _proj_kernel(x_ref, g_ref, wr_ref, wt_ref, gq_ref, wqn_ref, wqr_ref, wuk_ref, gkv_ref,
                 c64_ref, s64_ref, c32_ref, s32_ref, ct64_ref, st64_ref, ct32_ref, st32_ref,
                 *rest, layer):
    n_prev = len(rest) - 10
    (mq_ref, dq_ref, qm_ref, km_ref, ckv_ref,
     mkT_ref, mvT_ref, dkT_ref, dvT_ref, kpeT_ref) = rest[n_prev:]
    for prev_ref, out_ref in zip(rest[:n_prev], rest[n_prev + 4:]):
        out_ref[0:layer] = prev_ref[...]
    x = x_ref[0]
    h = _rms(x, g_ref[...], NORM_EPS).astype(BF16)
    zr = _mm(h, wr_ref[...])
    zt = _nt(wt_ref[...], h)

    c64, s64, c32, s32 = c64_ref[...], s64_ref[...], c32_ref[...], s32_ref[...]
    for c in range(D_MQ // LANES):
        mq_ref[0, :, c * LANES:(c + 1) * LANES] = _rope_rows(
            zr[:, _R_MQ + c * LANES:_R_MQ + (c + 1) * LANES], c64, s64, HEAD_DIM // 2)
    for c in range(D_DQK // LANES):
        dq_ref[0, :, c * LANES:(c + 1) * LANES] = _rope_rows(
            zr[:, _R_DQ + c * LANES:_R_DQ + (c + 1) * LANES], c32, s32, DIFF_QK_DIM // 2).astype(BF16)

    cqn = _rms(zr[:, _R_CQ:_R_CQ + MLA_Q_LORA], gq_ref[...], NORM_EPS).astype(BF16)
    qn = _mm(cqn, wqn_ref[...])
    qr = _mm(cqn, wqr_ref[...])
    for hd in range(MLA_HEADS):
        q_lat = _mm(qn[:, hd * MLA_NOPE_DIM:(hd + 1) * MLA_NOPE_DIM].astype(BF16), wuk_ref[hd])
        q_pe = _rope_rows(qr[:, hd * LANES:(hd + 1) * LANES], c32, s32, MLA_ROPE_DIM // 2)
        qm_ref[0, :, 2 * hd * LANES:(2 * hd + 1) * LANES] = q_lat.astype(BF16)
        qm_ref[0, :, (2 * hd + 1) * LANES:(2 * hd + 2) * LANES] = q_pe.astype(BF16)

    ckv = _rms(zr[:, _R_CKV:_R_CKV + MLA_KV_LORA], gkv_ref[...], NORM_EPS)
    ckv_ref[layer, 0] = ckv
    km_ref[0, :, 0:LANES] = ckv.astype(BF16)
    km_ref[0, :, LANES:2 * LANES] = _rope_rows(zr[:, _R_KPE:_R_KPE + LANES], c32, s32,
                                               MLA_ROPE_DIM // 2).astype(BF16)

    ct64, st64, ct32, st32 = ct64_ref[...], st64_ref[...], ct32_ref[...], st32_ref[...]

    def rope_cols(dst_ref, row0, n_groups, half, cos_t, sin_t):
        for gi in range(n_groups):
            r = row0 + gi * 2 * half
            x1, x2 = zt[r:r + half], zt[r + half:r + 2 * half]
            o = gi * 2 * half
            dst_ref[layer, 0, o:o + half, :] = x1 * cos_t - x2 * sin_t
            dst_ref[layer, 0, o + half:o + 2 * half, :] = x2 * cos_t + x1 * sin_t

    rope_cols(mkT_ref, _T_MK, MOBA_KV_HEADS, HEAD_DIM // 2, ct64, st64)
    mvT_ref[layer, 0] = zt[_T_MV:_T_MV + D_MKV]
    rope_cols(dkT_ref, _T_DK, DIFF_HEADS * 2, DIFF_QK_DIM // 2, ct32, st32)
    dvT_ref[layer, 0] = zt[_T_DV:_T_DV + D_DV]
    rope_cols(kpeT_ref, _T_KPE, 1, MLA_ROPE_DIM // 2, ct32, st32)


def _const_spec(shape):
    nd = len(shape)
    return pl.BlockSpec(shape, lambda *_: (0,) * nd)


def _projection(x, layer, wts, tabs, prev):
    B, S, D = x.shape
    tm = min(PROJ_ROWS, S)
    assert S % tm == 0
    grid = (B, S // tm)
    row_tab = pl.BlockSpec((tm, LANES), lambda b, i: (i, 0))
    in_specs = [
        pl.BlockSpec((1, tm, D), lambda b, i: (b, i, 0)),
        _const_spec((1, D)),
        _const_spec((D, R_COLS)), _const_spec((T_ROWS, D)),
        _const_spec((1, MLA_Q_LORA)),
        _const_spec((MLA_Q_LORA, MLA_HEADS * MLA_NOPE_DIM)),
        _const_spec((MLA_Q_LORA, MLA_HEADS * LANES)),
        _const_spec((MLA_HEADS, MLA_NOPE_DIM, MLA_KV_LORA)),
        _const_spec((1, MLA_KV_LORA)),
        row_tab, row_tab, row_tab, row_tab,
        pl.BlockSpec((HEAD_DIM // 2, tm), lambda b, i: (0, i)),
        pl.BlockSpec((HEAD_DIM // 2, tm), lambda b, i: (0, i)),
        pl.BlockSpec((DIFF_QK_DIM // 2, tm), lambda b, i: (0, i)),
        pl.BlockSpec((DIFF_QK_DIM // 2, tm), lambda b, i: (0, i)),
    ]
    args = [x, wts["g_attn"], wts["w_r"], wts["w_t"], wts["g_mla_q"], wts["w_uqn"], wts["w_uqr"],
            wts["w_uk"], wts["g_mla_kv"], *tabs]
    t_rows = (D_MKV, D_MKV, D_DQK, D_DV, MLA_ROPE_DIM)

    def stacked_specs(n):
        return ([pl.BlockSpec((n, 1, tm, MLA_KV_LORA), lambda b, i: (0, b, i, 0))]
                + [pl.BlockSpec((n, 1, r, tm), lambda b, i: (0, b, 0, i)) for r in t_rows])

    if prev is not None:
        in_specs += stacked_specs(layer)
        args += list(prev)
    n = layer + 1
    out_shape = [
        jax.ShapeDtypeStruct((B, S, D_MQ), F32),
        jax.ShapeDtypeStruct((B, S, D_DQK), BF16),
        jax.ShapeDtypeStruct((B, S, D_QM), BF16),
        jax.ShapeDtypeStruct((B, S, 2 * LANES), BF16),
        jax.ShapeDtypeStruct((n, B, S, MLA_KV_LORA), F32),
    ] + [jax.ShapeDtypeStruct((n, B, r, S), F32) for r in t_rows]
    out_specs = [
        pl.BlockSpec((1, tm, D_MQ), lambda b, i: (b, i, 0)),
        pl.BlockSpec((1, tm, D_DQK), lambda b, i: (b, i, 0)),
        pl.BlockSpec((1, tm, D_QM), lambda b, i: (b, i, 0)),
        pl.BlockSpec((1, tm, 2 * LANES), lambda b, i: (b, i, 0)),
    ] + stacked_specs(n)
    outs = pl.pallas_call(
        functools.partial(_proj_kernel, layer=layer),
        grid=grid, in_specs=in_specs, out_specs=out_specs, out_shape=out_shape,
        compiler_params=pltpu.CompilerParams(
            dimension_semantics=("arbitrary", "arbitrary"), vmem_limit_bytes=VMEM_LIMIT),
        name="projection",
    )(*args)
    return outs[:4], list(outs[4:])


def _causal_tile_mask(shape):
    return lax.broadcasted_iota(jnp.int32, shape, 1) <= lax.broadcasted_iota(jnp.int32, shape, 0)


def _flash_blocks(j, chains):
    state = []
    for own_scores, _, pv in chains:
        s = own_scores()
        m = jnp.max(s, axis=1, keepdims=True)
        p = jnp.exp(s - m)
        state.append((m, jnp.sum(p, axis=1, keepdims=True), pv(j, p.astype(BF16))))

    def body(n, carry):
        new = []
        for (m, l, acc), (_, past_scores, pv) in zip(carry, chains):
            s = past_scores(n)
            m_new = jnp.maximum(m, jnp.max(s, axis=1, keepdims=True))
            alpha = jnp.exp(m - m_new)
            p = jnp.exp(s - m_new)
            new.append((m_new, alpha * l + jnp.sum(p, axis=1, keepdims=True),
                        alpha * acc + pv(n, p.astype(BF16))))
        return tuple(new)

    state = lax.fori_loop(0, j, body, tuple(state))
    return [acc / l for _, l, acc in state]


def _moba_prompt_kernel(mq_ref, kT_ref, vT_ref, o_ref, kaug_ref, vdup_ref, kmean_ref, *, nb):
    j = pl.program_id(1)
    blk = MOBA_BLOCK
    lane = lax.broadcasted_iota(jnp.int32, (blk, LANES), 1)

    @pl.when(j == 0)
    def _():
        kt = kT_ref[0, 0]
        vt = vT_ref[0, 0]
        row = lax.broadcasted_iota(jnp.int32, (LANES, blk), 0)
        lane_k = lax.broadcasted_iota(jnp.int32, (D_MKV, LANES), 1)
        kmean = jnp.zeros((D_MKV, LANES), F32)
        for n in range(nb):
            kb = kt[:, n * blk:(n + 1) * blk]
            vb = vt[:, n * blk:(n + 1) * blk]
            ind = jnp.where(row == n, 1.0, 0.0).astype(BF16)
            for g in range(MOBA_KV_HEADS):
                kg = kb[g * HEAD_DIM:(g + 1) * HEAD_DIM].astype(BF16)
                vg = vb[g * HEAD_DIM:(g + 1) * HEAD_DIM].astype(BF16)
                kaug_ref[g * nb + n] = jnp.concatenate([kg, kg, ind], axis=0)
                vdup_ref[g * nb + n] = jnp.concatenate([vg, vg], axis=0)
            kmean = jnp.where(lane_k == n, jnp.mean(kb, axis=1, keepdims=True), kmean)
        kmean_ref[...] = kmean

    q = mq_ref[0]
    kmean = kmean_ref[...]
    lane_f = lane.astype(F32)
    valid = lane < j
    n_sel = min(MOBA_TOPK, nb - 1)
    scale = HEAD_DIM ** -0.5
    causal = _causal_tile_mask((blk, blk))
    chains = []
    for c in range(D_MQ // LANES):
        qc = q[:, c * LANES:(c + 1) * LANES]
        for hh in range(2):
            g = (2 * c + hh) // MOBA_GROUP
            in_half = (lane < HEAD_DIM) if hh == 0 else (lane >= HEAD_DIM)
            qh = jnp.where(in_half, qc, 0.0)
            kmg = kmean[g * HEAD_DIM:(g + 1) * HEAD_DIM]
            gate = jnp.dot(qh, jnp.concatenate([kmg, kmg], axis=0),
                           precision=lax.Precision.HIGHEST, preferred_element_type=F32)
            gate = jnp.where(valid, gate, -jnp.inf)
            bias = jnp.where(valid, NEG_INF, 0.0)
            for _ in range(n_sel):
                mx = jnp.max(gate, axis=1, keepdims=True)
                first = jnp.min(jnp.where(gate == mx, lane_f, float(LANES)), axis=1, keepdims=True)
                pick = lane_f == first
                bias = jnp.where(pick, 0.0, bias)
                gate = jnp.where(pick, -jnp.inf, gate)
            qa = jnp.concatenate([(qh * scale).astype(BF16), bias.astype(BF16)], axis=1)

            def own_scores(qa=qa, g=g):
                return jnp.where(causal, _mm(qa, kaug_ref[g * nb + j]), NEG_INF)

            def past_scores(n, qa=qa, g=g):
                return _mm(qa, kaug_ref[g * nb + n])

            def pv(n, p, g=g):
                return _nt(p, vdup_ref[g * nb + n])

            chains.append((own_scores, past_scores, pv))
    outs = _flash_blocks(j, chains)
    for c in range(D_MQ // LANES):
        o_ref[0, :, c * LANES:(c + 1) * LANES] = jnp.where(
            lane < HEAD_DIM, outs[2 * c], outs[2 * c + 1]).astype(BF16)


def _moba_prompt(mq, mkT, mvT, layer):
    B, S, _ = mq.shape
    assert S % MOBA_BLOCK == 0
    nb = S // MOBA_BLOCK
    kv_spec = pl.BlockSpec((1, 1, D_MKV, S), lambda b, j: (layer, b, 0, 0))
    return pl.pallas_call(
        functools.partial(_moba_prompt_kernel, nb=nb),
        grid=(B, nb),
        in_specs=[pl.BlockSpec((1, MOBA_BLOCK, D_MQ), lambda b, j: (b, j, 0)), kv_spec, kv_spec],
        out_specs=pl.BlockSpec((1, MOBA_BLOCK, D_MQ), lambda b, j: (b, j, 0)),
        out_shape=jax.ShapeDtypeStruct((B, S, D_MQ), BF16),
        scratch_shapes=[
            pltpu.VMEM((MOBA_KV_HEADS * nb, 2 * LANES, MOBA_BLOCK), BF16),
            pltpu.VMEM((MOBA_KV_HEADS * nb, LANES, MOBA_BLOCK), BF16),
            pltpu.VMEM((D_MKV, LANES), F32),
        ],
        compiler_params=pltpu.CompilerParams(
            dimension_semantics=("arbitrary", "arbitrary"), vmem_limit_bytes=VMEM_LIMIT),
        name="moba_prompt",
    )(mq, mkT, mvT)


def _diff_lambda(lq1_ref, lk1_ref, lq2_ref, lk2_ref, lam_init):
    return (jnp.exp(jnp.sum(lq1_ref[...] * lk1_ref[...], axis=1, keepdims=True))
            - jnp.exp(jnp.sum(lq2_ref[...] * lk2_ref[...], axis=1, keepdims=True)) + lam_init)


def _diff_prompt_kernel(dq_ref, kT_ref, vT_ref, lq1_ref, lk1_ref, lq2_ref, lk2_ref, gsub_ref,
                        o_ref, kbf_ref, vbf_ref, *, nk, lam_init):
    j = pl.program_id(1)
    blk = ATTN_BLK

    @pl.when(j == 0)
    def _():
        for n in range(nk):
            kbf_ref[n] = kT_ref[0, 0, :, n * blk:(n + 1) * blk].astype(BF16)
            vbf_ref[n] = vT_ref[0, 0, :, n * blk:(n + 1) * blk].astype(BF16)

    lam = _diff_lambda(lq1_ref, lk1_ref, lq2_ref, lk2_ref, lam_init)
    q = dq_ref[0]
    lane = lax.broadcasted_iota(jnp.int32, (blk, LANES), 1)
    causal = _causal_tile_mask((blk, blk))
    scale = DIFF_QK_DIM ** -0.5
    zero = jnp.zeros((), BF16)
    chains = []
    for c in range(D_DQK // LANES):
        qc = q[:, c * LANES:(c + 1) * LANES]
        rows = slice(c * LANES, (c + 1) * LANES)
        for sub in range(4):
            lo = sub * DIFF_QK_DIM
            qm = jnp.where(jnp.logical_and(lane >= lo, lane < lo + DIFF_QK_DIM), qc, zero)

            def own_scores(qm=qm, rows=rows):
                return jnp.where(causal, _mm(qm, kbf_ref[j, rows, :]) * scale, NEG_INF)

            def past_scores(n, qm=qm, rows=rows):
                return _mm(qm, kbf_ref[n, rows, :]) * scale

            def pv(n, p, rows=rows):
                return _nt(p, vbf_ref[n, rows, :])

            chains.append((own_scores, past_scores, pv))
    maps = _flash_blocks(j, chains)
    for c in range(D_DQK // LANES):
        normed = []
        for hh in range(2):
            o_h = maps[4 * c + 2 * hh] - lam * maps[4 * c + 2 * hh + 1]
            in_half = (lane < DIFF_V_DIM) if hh == 0 else (lane >= DIFF_V_DIM)
            ms = jnp.sum(jnp.where(in_half, o_h * o_h, 0.0), axis=1, keepdims=True) * (1.0 / DIFF_V_DIM)
            normed.append(o_h * lax.rsqrt(ms + SUBLN_EPS))
        out = jnp.where(lane < DIFF_V_DIM, normed[0], normed[1]) * gsub_ref[...] * (1.0 - lam_init)
        o_ref[0, :, c * LANES:(c + 1) * LANES] = out.astype(BF16)


def _diff_prompt(dq, dkT, dvT, lams, gsub2, layer, lam_init):
    B, S, _ = dq.shape
    assert S % ATTN_BLK == 0
    nk = S // ATTN_BLK
    kv_spec = pl.BlockSpec((1, 1, D_DQK, S), lambda b, j: (layer, b, 0, 0))
    lam_spec = _const_spec((1, DIFF_QK_DIM))
    return pl.pallas_call(
        functools.partial(_diff_prompt_kernel, nk=nk, lam_init=lam_init),
        grid=(B, nk),
        in_specs=[pl.BlockSpec((1, ATTN_BLK, D_DQK), lambda b, j: (b, j, 0)), kv_spec, kv_spec,
                  lam_spec, lam_spec, lam_spec, lam_spec, _const_spec((1, LANES))],
        out_specs=pl.BlockSpec((1, ATTN_BLK, D_DV), lambda b, j: (b, j, 0)),
        out_shape=jax.ShapeDtypeStruct((B, S, D_DV), BF16),
        scratch_shapes=[pltpu.VMEM((nk, D_DQK, ATTN_BLK), BF16), pltpu.VMEM((nk, D_DV, ATTN_BLK), BF16)],
        compiler_params=pltpu.CompilerParams(
            dimension_semantics=("arbitrary", "arbitrary"), vmem_limit_bytes=VMEM_LIMIT),
        name="diff_prompt",
    )(dq, dkT, dvT, *lams, gsub2)


def _mla_prompt_kernel(qm_ref, km_ref, o_ref):
    j = pl.program_id(1)
    blk = ATTN_BLK
    causal = _causal_tile_mask((blk, blk))
    scale = (MLA_NOPE_DIM + MLA_ROPE_DIM) ** -0.5
    q = qm_ref[0]

    def keys(n):
        return km_ref[0, pl.ds(pl.multiple_of(n * blk, blk), blk), :]

    def pv(n, p):
        return _mm(p, km_ref[0, pl.ds(pl.multiple_of(n * blk, blk), blk), 0:MLA_KV_LORA])

    chains = []
    for hd in range(MLA_HEADS):
        qh = q[:, hd * 2 * LANES:(hd + 1) * 2 * LANES]

        def own_scores(qh=qh):
            return jnp.where(causal, _nt(qh, keys(j)) * scale, NEG_INF)

        def past_scores(n, qh=qh):
            return _nt(qh, keys(n)) * scale

        chains.append((own_scores, past_scores, pv))
    for hd, o in enumerate(_flash_blocks(j, chains)):
        o_ref[0, :, hd * MLA_KV_LORA:(hd + 1) * MLA_KV_LORA] = o.astype(BF16)


def _mla_prompt(qm, km):
    B, S, _ = qm.shape
    assert S % ATTN_BLK == 0
    return pl.pallas_call(
        _mla_prompt_kernel,
        grid=(B, S // ATTN_BLK),
        in_specs=[pl.BlockSpec((1, ATTN_BLK, D_QM), lambda b, j: (b, j, 0)),
                  pl.BlockSpec((1, S, 2 * LANES), lambda b, j: (b, 0, 0))],
        out_specs=pl.BlockSpec((1, ATTN_BLK, D_LAT), lambda b, j: (b, j, 0)),
        out_shape=jax.ShapeDtypeStruct((B, S, D_LAT), BF16),
        compiler_params=pltpu.CompilerParams(
            dimension_semantics=("arbitrary", "arbitrary"), vmem_limit_bytes=VMEM_LIMIT),
        name="mla_prompt",
    )(qm, km)


def _mix_mlp_kernel(x_ref, om_ref, od_ref, ol_ref, wuv_ref, wo_ref, gm_ref, wup_ref, wdn_ref, gf_ref,
                    o_ref, *, final):
    attn = _mm(om_ref[...], wo_ref[0:D_MQ, :]) + _mm(od_ref[...], wo_ref[D_MQ:D_MQ + D_DV, :])
    ol = ol_ref[...]
    for c in range(MLA_HEADS // 2):
        o_mla = (_mm(ol[:, 2 * c * LANES:(2 * c + 1) * LANES], wuv_ref[2 * c])
                 + _mm(ol[:, (2 * c + 1) * LANES:(2 * c + 2) * LANES], wuv_ref[2 * c + 1]))
        r0 = D_MQ + D_DV + c * LANES
        attn = attn + _mm(o_mla.astype(BF16), wo_ref[r0:r0 + LANES, :])
    x1 = x_ref[...] + attn
    h = _rms(x1, gm_ref[...], NORM_EPS).astype(BF16)
    d_ff = wup_ref.shape[1]
    down = jnp.zeros_like(x1)
    for c in range(d_ff // FF_CHUNK):
        up = jnp.maximum(_mm(h, wup_ref[:, c * FF_CHUNK:(c + 1) * FF_CHUNK]), 0.0)
        down = down + _mm((up * up).astype(BF16), wdn_ref[c * FF_CHUNK:(c + 1) * FF_CHUNK, :])
    x2 = x1 + down
    o_ref[...] = _rms(x2, gf_ref[...], NORM_EPS) if final else x2


def _mix_mlp(x, om, od, ol, wts, g_final, final):
    N, D = x.shape
    tm = min(PROJ_ROWS, N)
    assert N % tm == 0
    d_ff = wts["w_up"].shape[1]
    assert d_ff % FF_CHUNK == 0

    def resident(shape):
        nd = len(shape)
        return pl.BlockSpec(shape, lambda i: (0,) * nd, pipeline_mode=pl.Buffered(1))

    def rows(width):
        return pl.BlockSpec((tm, width), lambda i: (i, 0))

    return pl.pallas_call(
        functools.partial(_mix_mlp_kernel, final=final),
        grid=(N // tm,),
        in_specs=[rows(D), rows(D_MQ), rows(D_DV), rows(D_LAT),
                  resident((MLA_HEADS, MLA_KV_LORA, LANES)), resident((D, D)), resident((1, D)),
                  resident((D, d_ff)), resident((d_ff, D)), resident((1, D))],
        out_specs=rows(D),
        out_shape=jax.ShapeDtypeStruct((N, D), F32),
        compiler_params=pltpu.CompilerParams(
            dimension_semantics=("arbitrary",), vmem_limit_bytes=VMEM_LIMIT),
        name="mix_mlp",
    )(x, om, od, ol, wts["w_uv"], wts["w_out"], wts["g_mlp"], wts["w_up"], wts["w_down"], g_final)


def _page_copies(pt_ref, seq, slot, pools, bufs, sems, layer, n_pages):
    copies = []
    for k, (pool, buf) in enumerate(zip(pools, bufs)):
        for p in range(n_pages):
            copies.append(pltpu.make_async_copy(
                pool.at[layer, pt_ref[seq, p]], buf.at[slot, p], sems.at[k, slot]))
    return copies


def _fetch_pages(pt_ref, pools, bufs, sems, layer, n_pages):
    b = pl.program_id(0)
    slot = lax.rem(b, 2)

    @pl.when(b == 0)
    def _():
        for cp in _page_copies(pt_ref, 0, 0, pools, bufs, sems, layer, n_pages):
            cp.start()

    @pl.when(b + 1 < pl.num_programs(0))
    def _():
        for cp in _page_copies(pt_ref, b + 1, 1 - slot, pools, bufs, sems, layer, n_pages):
            cp.start()

    for cp in _page_copies(pt_ref, b, slot, pools, bufs, sems, layer, n_pages):
        cp.wait()
    return slot


def _new_token_mask(n_rows, t_new):
    assert t_new & (t_new - 1) == 0
    row = lax.broadcasted_iota(jnp.int32, (n_rows, LANES), 0)
    lane = lax.broadcasted_iota(jnp.int32, (n_rows, LANES), 1)
    return lane <= (row & (t_new - 1))


def _softmax_pages(s_ref):
    s = s_ref[...]
    m = jnp.max(jnp.max(s, axis=0), axis=1, keepdims=True)
    p = jnp.exp(s - m[None])
    l = jnp.sum(jnp.sum(p, axis=0), axis=1, keepdims=True)
    s_ref[...] = p / l[None]


def _page_loop(n, body, init):
    return lax.fori_loop(0, n, body, init, unroll=PAGE_UNROLL)


def _moba_sample_kernel(pt_ref, q_ref, kn_ref, vn_ref, kpool, vpool, o_ref, kbuf, vbuf, sems, s_ref,
                        *, layer, n_pages, t_new):
    slot = _fetch_pages(pt_ref, (kpool, vpool), (kbuf, vbuf), sems, layer, n_pages)
    kbuf[slot, n_pages] = kn_ref[0]
    vbuf[slot, n_pages] = vn_ref[0]
    q = q_ref[0]
    n_rows = q.shape[0]
    qb = (q * HEAD_DIM ** -0.5).astype(BF16)

    def scores(p, _):
        s_ref[p] = _mm(qb, kbuf[slot, p].astype(BF16))
        return 0

    _page_loop(n_pages + 1, scores, 0)

    ppb = MOBA_BLOCK // LANES
    n_blocks = n_pages // ppb
    lane_k = lax.broadcasted_iota(jnp.int32, (D_MKV, LANES), 1)

    def block_mean(n, kmean):
        blk_sum = kbuf[slot, n * ppb]
        for r in range(1, ppb):
            blk_sum = blk_sum + kbuf[slot, n * ppb + r]
        return jnp.where(lane_k == n, jnp.sum(blk_sum, axis=1, keepdims=True) * (1.0 / MOBA_BLOCK), kmean)

    kmean = _page_loop(n_blocks, block_mean, jnp.zeros((D_MKV, LANES), F32))
    gate = jnp.dot(q, kmean, precision=lax.Precision.HIGHEST, preferred_element_type=F32)
    lane = lax.broadcasted_iota(jnp.int32, (n_rows, LANES), 1)
    lane_f = lane.astype(F32)
    valid = lane < n_blocks
    gate = jnp.where(valid, gate, -jnp.inf)
    sel = jnp.zeros((n_rows, LANES), F32)
    for _ in range(min(MOBA_TOPK, n_blocks)):
        mx = jnp.max(gate, axis=1, keepdims=True)
        first = jnp.min(jnp.where(gate == mx, lane_f, float(LANES)), axis=1, keepdims=True)
        pick = lane_f == first
        sel = jnp.where(jnp.logical_and(pick, valid), 1.0, sel)
        gate = jnp.where(pick, -jnp.inf, gate)

    def mask_block(n, _):
        chosen = jnp.sum(jnp.where(lane == n, sel, 0.0), axis=1, keepdims=True) > 0.5
        for r in range(ppb):
            s_ref[n * ppb + r] = jnp.where(chosen, s_ref[n * ppb + r], NEG_INF)
        return 0

    _page_loop(n_blocks, mask_block, 0)
    s_ref[n_pages] = jnp.where(_new_token_mask(n_rows, t_new), s_ref[n_pages], NEG_INF)
    _softmax_pages(s_ref)

    def pv(p, acc):
        return acc + _nt(s_ref[p].astype(BF16), vbuf[slot, p].astype(BF16))

    o_ref[0] = _page_loop(n_pages + 1, pv, jnp.zeros((n_rows, D_MKV), F32))


def _diff_sample_kernel(pt_ref, q_ref, kn_ref, vn_ref, lq1_ref, lk1_ref, lq2_ref, lk2_ref, gsub_ref,
                        kpool, vpool, o_ref, kbuf, vbuf, sems, s_ref,
                        *, layer, n_pages, t_new, lam_init):
    slot = _fetch_pages(pt_ref, (kpool, vpool), (kbuf, vbuf), sems, layer, n_pages)
    kbuf[slot, n_pages] = kn_ref[0]
    vbuf[slot, n_pages] = vn_ref[0]
    q = q_ref[0]
    n_rows = q.shape[0]
    scale = DIFF_QK_DIM ** -0.5

    def scores(p, _):
        s_ref[p] = _mm(q, kbuf[slot, p].astype(BF16)) * scale
        return 0

    _page_loop(n_pages + 1, scores, 0)
    s_ref[n_pages] = jnp.where(_new_token_mask(n_rows, t_new), s_ref[n_pages], NEG_INF)
    _softmax_pages(s_ref)
    lam = _diff_lambda(lq1_ref, lk1_ref, lq2_ref, lk2_ref, lam_init)
    half = n_rows // 2

    def pv(p, acc):
        a = (s_ref[p, 0:half, :] - lam * s_ref[p, half:n_rows, :]).astype(BF16)
        return acc + _nt(a, vbuf[slot, p].astype(BF16))

    o = _page_loop(n_pages + 1, pv, jnp.zeros((half, D_DV), F32))
    assert t_new & (t_new - 1) == 0
    row = lax.broadcasted_iota(jnp.int32, o.shape, 0)
    lane = lax.broadcasted_iota(jnp.int32, o.shape, 1)
    own = (row >> (t_new.bit_length() - 1)) == (lane >> (DIFF_V_DIM.bit_length() - 1))
    ms = jnp.sum(jnp.where(own, o * o, 0.0), axis=1, keepdims=True) * (1.0 / DIFF_V_DIM)
    o_ref[0] = jnp.where(own, o * lax.rsqrt(ms + SUBLN_EPS), 0.0) * gsub_ref[...] * (1.0 - lam_init)


def _mla_sample_kernel(pt_ref, ql_ref, qp_ref, cn_ref, pn_ref, cpool, ppool, o_ref, cbuf, pbuf, sems, s_ref,
                       *, layer, n_pages, t_new):
    slot = _fetch_pages(pt_ref, (cpool, ppool), (cbuf, pbuf), sems, layer, n_pages)
    cbuf[slot, n_pages] = cn_ref[0]
    pbuf[slot, n_pages] = pn_ref[0]
    ql, qp = ql_ref[0], qp_ref[0]
    n_rows = ql.shape[0]
    scale = (MLA_NOPE_DIM + MLA_ROPE_DIM) ** -0.5

    def scores(p, _):
        s_ref[p] = (_nt(ql, cbuf[slot, p].astype(BF16)) + _mm(qp, pbuf[slot, p].astype(BF16))) * scale
        return 0

    _page_loop(n_pages + 1, scores, 0)
    s_ref[n_pages] = jnp.where(_new_token_mask(n_rows, t_new), s_ref[n_pages], NEG_INF)
    _softmax_pages(s_ref)

    def pv(p, acc):
        return acc + _mm(s_ref[p].astype(BF16), cbuf[slot, p].astype(BF16))

    o_ref[0] = _page_loop(n_pages + 1, pv, jnp.zeros((n_rows, MLA_KV_LORA), F32))


def _paged_call(kernel, page_table, per_seq, consts, pools, n_rows, out_rows, out_lanes, page_shapes, name):
    Bd, n_pages = page_table.shape
    in_specs = [pl.BlockSpec((1,) + a.shape[1:], lambda b, pt: (b, 0, 0)) for a in per_seq]
    in_specs += [pl.BlockSpec(a.shape, lambda b, pt: (0, 0)) for a in consts]
    in_specs += [pl.BlockSpec(memory_space=pl.ANY) for _ in pools]
    scratch = [pltpu.VMEM((2, n_pages + 1) + ps, F32) for ps in page_shapes]
    scratch.append(pltpu.SemaphoreType.DMA((len(pools), 2)))
    scratch.append(pltpu.VMEM((n_pages + 1, n_rows, LANES), F32))
    return pl.pallas_call(
        kernel,
        grid_spec=pltpu.PrefetchScalarGridSpec(
            num_scalar_prefetch=1, grid=(Bd,), in_specs=in_specs,
            out_specs=pl.BlockSpec((1, out_rows, out_lanes), lambda b, pt: (b, 0, 0)),
            scratch_shapes=scratch),
        out_shape=jax.ShapeDtypeStruct((Bd, out_rows, out_lanes), F32),
        compiler_params=pltpu.CompilerParams(
            dimension_semantics=("arbitrary",), vmem_limit_bytes=VMEM_LIMIT),
        name=name,
    )(page_table, *per_seq, *consts, *pools)


def _pad_lanes(a):
    return jnp.pad(a, ((0, 0), (0, 0), (0, LANES - a.shape[2])))


def _sample_attention(acts, new_rows, layer, Bd, T, page_table, pools, lams, gsub, lam_init):
    mq, dq, qm, _ = acts
    ckv_n, mkT, mvT, dkT, dvT, kpeT = new_rows
    n_pages = page_table.shape[1]
    assert (n_pages * LANES) % MOBA_BLOCK == 0
    kw = dict(layer=layer, n_pages=n_pages, t_new=T)

    def per_seq_cols(a):
        f = a.shape[1]
        return _pad_lanes(a[0].reshape(f, Bd, T).transpose(1, 0, 2))

    q = mq[0].reshape(Bd, T, MOBA_HEADS, HEAD_DIM).transpose(0, 2, 1, 3)
    z = jnp.zeros_like(q[:, :MOBA_GROUP])
    q_moba = jnp.concatenate([jnp.concatenate([q[:, :MOBA_GROUP], z], axis=-1),
                              jnp.concatenate([z, q[:, MOBA_GROUP:]], axis=-1)], axis=1)
    q_moba = q_moba.reshape(Bd, MOBA_HEADS * T, LANES)
    o = _paged_call(
        functools.partial(_moba_sample_kernel, **kw), page_table,
        [q_moba, per_seq_cols(mkT[layer]), per_seq_cols(mvT[layer])], [], pools[0:2],
        MOBA_HEADS * T, MOBA_HEADS * T, LANES, [(D_MKV, LANES), (D_MKV, LANES)], "moba_sample")
    o = o.reshape(Bd, MOBA_HEADS, T, MOBA_KV_HEADS, HEAD_DIM)
    om = jnp.stack([o[:, h, :, h // MOBA_GROUP] for h in range(MOBA_HEADS)], axis=2)
    om = om.reshape(Bd * T, D_MQ).astype(BF16)

    q = dq[0].reshape(Bd, T, DIFF_HEADS, 2, DIFF_QK_DIM)
    q_diff = jnp.zeros((Bd, 2, DIFF_HEADS, T, D_DQK), BF16)
    for h in range(DIFF_HEADS):
        for c in range(2):
            lo = (2 * h + c) * DIFF_QK_DIM
            q_diff = q_diff.at[:, c, h, :, lo:lo + DIFF_QK_DIM].set(q[:, :, h, c])
    q_diff = q_diff.reshape(Bd, 2 * DIFF_HEADS * T, D_DQK)
    o = _paged_call(
        functools.partial(_diff_sample_kernel, lam_init=lam_init, **kw), page_table,
        [q_diff, per_seq_cols(dkT[layer]), per_seq_cols(dvT[layer])], [*lams, gsub], pools[2:4],
        2 * DIFF_HEADS * T, DIFF_HEADS * T, D_DV, [(D_DQK, LANES), (D_DV, LANES)], "diff_sample")
    o = o.reshape(Bd, DIFF_HEADS, T, DIFF_HEADS, DIFF_V_DIM)
    od = jnp.stack([o[:, h, :, h] for h in range(DIFF_HEADS)], axis=2).reshape(Bd * T, D_DV).astype(BF16)

    q = qm[0].reshape(Bd, T, MLA_HEADS, 2 * LANES).transpose(0, 2, 1, 3).reshape(Bd, MLA_HEADS * T, 2 * LANES)
    ckv_new = jnp.pad(ckv_n[layer, 0].reshape(Bd, T, MLA_KV_LORA), ((0, 0), (0, LANES - T), (0, 0)))
    o = _paged_call(
        functools.partial(_mla_sample_kernel, **kw), page_table,
        [q[:, :, :MLA_KV_LORA], q[:, :, LANES:LANES + MLA_ROPE_DIM], ckv_new, per_seq_cols(kpeT[layer])],
        [], pools[4:6], MLA_HEADS * T, MLA_HEADS * T, MLA_KV_LORA,
        [(LANES, MLA_KV_LORA), (MLA_ROPE_DIM, LANES)], "mla_sample")
    ol = o.reshape(Bd, MLA_HEADS, T, MLA_KV_LORA).transpose(0, 2, 1, 3).reshape(Bd * T, D_LAT).astype(BF16)
    return om, od, ol


def _rope_tables(pos):
    out_rows, out_cols = [], []
    for d in (HEAD_DIM, DIFF_QK_DIM):
        half = d // 2
        inv = 1.0 / (ROPE_THETA ** (jnp.arange(half, dtype=F32) * (2.0 / d)))
        ang = pos.astype(F32)[:, None] * inv[None, :]
        cos, sin = jnp.cos(ang), jnp.sin(ang)
        out_rows += [jnp.tile(cos, (1, LANES // half)),
                     jnp.tile(jnp.concatenate([-sin, sin], axis=1), (1, LANES // d))]
        out_cols += [cos.T, sin.T]
    return out_rows + out_cols


def _layer_weights(l, g_attn, w_in, g_mla_q, w_mla_uq, g_mla_kv, w_mla_uk, w_mla_uv, w_out, g_mlp,
                   w_up, w_down):
    w = w_in[l]
    d = w.shape[0]
    w_r = jnp.concatenate([w[:, _O_MQ:_O_MK], w[:, _O_DQ:_O_DK], w[:, _O_CQ:_O_KPE], w[:, _O_KPE:_O_END],
                           jnp.zeros((d, LANES - MLA_ROPE_DIM), w.dtype)], axis=1)
    w_t = jnp.concatenate([w[:, _O_MK:_O_DQ], w[:, _O_DK:_O_CQ], w[:, _O_KPE:_O_END]], axis=1).T
    uq = w_mla_uq[l].reshape(MLA_Q_LORA, MLA_HEADS, MLA_NOPE_DIM + MLA_ROPE_DIM)
    w_uqn = uq[:, :, :MLA_NOPE_DIM].reshape(MLA_Q_LORA, MLA_HEADS * MLA_NOPE_DIM)
    w_uqr = jnp.pad(uq[:, :, MLA_NOPE_DIM:], ((0, 0), (0, 0), (0, LANES - MLA_ROPE_DIM)))
    w_uqr = w_uqr.reshape(MLA_Q_LORA, MLA_HEADS * LANES)
    w_uk = w_mla_uk[l].transpose(1, 2, 0)
    uv = w_mla_uv[l].transpose(1, 0, 2)
    w_uv = jnp.stack([jnp.pad(uv[h], ((0, 0), ((h % 2) * MLA_V_DIM, LANES - MLA_V_DIM - (h % 2) * MLA_V_DIM)))
                      for h in range(MLA_HEADS)], axis=0)
    return {
        "g_attn": g_attn[l][None], "w_r": w_r.astype(BF16), "w_t": w_t.astype(BF16),
        "g_mla_q": g_mla_q[l][None], "w_uqn": w_uqn.astype(BF16), "w_uqr": w_uqr.astype(BF16),
        "w_uk": w_uk.astype(BF16), "g_mla_kv": g_mla_kv[l][None], "w_uv": w_uv.astype(BF16),
        "w_out": w_out[l].astype(BF16), "g_mlp": g_mlp[l][None],
        "w_up": w_up[l].astype(BF16), "w_down": w_down[l].astype(BF16),
    }


def _feature_major_pool(cache, n_lead):
    nd = cache.ndim
    perm = (0, 1) + tuple(range(3, nd)) + (2,)
    t = cache.transpose(perm)
    return t.reshape(t.shape[0], t.shape[1], -1, t.shape[-1])


def kernel(x_prompt, x_sample, cache_moba_k, cache_moba_v, cache_diff_k, cache_diff_v, cache_mla_ckv, cache_mla_kpe, page_table, g_attn, w_in, g_mla_q, w_mla_uq, g_mla_kv, w_mla_uk, w_mla_uv, lambda_q1, lambda_k1, lambda_q2, lambda_k2, g_diff_sub, w_out, g_mlp, w_up, w_down, g_final):
    B, S, D = x_prompt.shape
    Bd, T, _ = x_sample.shape
    depth = w_in.shape[0]
    n_pages = page_table.shape[1]
    past_len = n_pages * cache_moba_k.shape[2]
    assert cache_moba_k.shape[2] == LANES

    tabs_p = _rope_tables(jnp.arange(S))
    tabs_s = _rope_tables(past_len + (jnp.arange(Bd * T) % T))
    pools = [_feature_major_pool(cache_moba_k, 2), _feature_major_pool(cache_moba_v, 2),
             _feature_major_pool(cache_diff_k, 2), _feature_major_pool(cache_diff_v, 2),
             cache_mla_ckv, _feature_major_pool(cache_mla_kpe, 2)]
    gf = g_final[None]

    xp = x_prompt
    xs = x_sample.reshape(1, Bd * T, D)
    rows_p = rows_s = None
    for l in range(depth):
        final = l == depth - 1
        lam_init = 0.8 - 0.6 * math.exp(-0.3 * l)
        wts = _layer_weights(l, g_attn, w_in, g_mla_q, w_mla_uq, g_mla_kv, w_mla_uk, w_mla_uv, w_out,
                             g_mlp, w_up, w_down)
        lams = [lambda_q1[l][None], lambda_k1[l][None], lambda_q2[l][None], lambda_k2[l][None]]
        gsub = g_diff_sub[l]
        gsub2 = jnp.tile(gsub, LANES // DIFF_V_DIM)[None]
        gsub4 = jnp.tile(gsub, DIFF_HEADS)[None]

        (mq, dq, qm, km), rows_p = _projection(xp, l, wts, tabs_p, rows_p)
        om = _moba_prompt(mq, rows_p[1], rows_p[2], l)
        od = _diff_prompt(dq, rows_p[3], rows_p[4], lams, gsub2, l, lam_init)
        ol = _mla_prompt(qm, km)
        xp = _mix_mlp(xp.reshape(B * S, D), om.reshape(B * S, D_MQ), od.reshape(B * S, D_DV),
                      ol.reshape(B * S, D_LAT), wts, gf, final).reshape(B, S, D)

        acts, rows_s = _projection(xs, l, wts, tabs_s, rows_s)
        om, od, ol = _sample_attention(acts, rows_s, l, Bd, T, page_table, pools, lams, gsub4, lam_init)
        xs = _mix_mlp(xs.reshape(Bd * T, D), om, od, ol, wts, gf, final).reshape(1, Bd * T, D)

    def finish(rows, b, s):
        ckv, mkT, mvT, dkT, dvT, kpeT = rows
        mk = mkT.reshape(depth, b, MOBA_KV_HEADS, HEAD_DIM, s).transpose(0, 1, 4, 2, 3)
        mv = mvT.reshape(depth, b, MOBA_KV_HEADS, HEAD_DIM, s).transpose(0, 1, 4, 2, 3)
        dk = dkT.reshape(depth, b, DIFF_HEADS, 2, DIFF_QK_DIM, s).transpose(0, 1, 5, 2, 3, 4)
        dv = dvT.reshape(depth, b, DIFF_HEADS, DIFF_V_DIM, s).transpose(0, 1, 4, 2, 3)
        return mk, mv, dk, dv, ckv, kpeT.transpose(0, 1, 3, 2)

    outs_p = finish(rows_p, B, S)
    outs_s = tuple(a.reshape((depth, Bd, T) + a.shape[3:]) for a in finish(rows_s, 1, Bd * T))
    return (xp, xs.reshape(Bd, T, D)) + outs_p + outs_s
```

```python
import functools
import math

import jax
import jax.numpy as jnp
from jax import lax
from jax.experimental import pallas as pl
from jax.experimental.pallas import tpu as pltpu

F32 = jnp.float32
BF16 = jnp.bfloat16

HEAD_DIM = 64
MOBA_HEADS = 6
MOBA_KV_HEADS = 2
MOBA_GROUP = MOBA_HEADS // MOBA_KV_HEADS
MOBA_BLOCK = 256
MOBA_TOPK = 3
DIFF_HEADS = 4
DIFF_QK_DIM = 32
DIFF_V_DIM = 2 * DIFF_QK_DIM
MLA_HEADS = 6
MLA_Q_LORA = 256
MLA_KV_LORA = 128
MLA_NOPE_DIM = 64
MLA_ROPE_DIM = 32
MLA_V_DIM = 64
ROPE_THETA = 10000.0
NORM_EPS = 1e-6
SUBLN_EPS = 1e-5
NEG_INF = -1e30

LANES = 128
ATTN_BLK = 256
PROJ_ROWS = 512
FF_CHUNK = 1024
PAGES_PER_TILE = 2
KEY_TILE = PAGES_PER_TILE * LANES
TILE_UNROLL = 4
VMEM_LIMIT = 48 * 1024 * 1024

_IN_SIZES = (MOBA_HEADS * HEAD_DIM, MOBA_KV_HEADS * HEAD_DIM, MOBA_KV_HEADS * HEAD_DIM,
             DIFF_HEADS * 2 * DIFF_QK_DIM, DIFF_HEADS * 2 * DIFF_QK_DIM, DIFF_HEADS * DIFF_V_DIM,
             MLA_Q_LORA, MLA_KV_LORA, MLA_ROPE_DIM)
_OFF = [0]
for _s in _IN_SIZES:
    _OFF.append(_OFF[-1] + _s)
(_O_MQ, _O_MK, _O_MV, _O_DQ, _O_DK, _O_DV, _O_CQ, _O_CKV, _O_KPE, _O_END) = _OFF

D_MQ = MOBA_HEADS * HEAD_DIM
D_MKV = MOBA_KV_HEADS * HEAD_DIM
D_DQK = DIFF_HEADS * 2 * DIFF_QK_DIM
D_DV = DIFF_HEADS * DIFF_V_DIM
D_QM = MLA_HEADS * 2 * LANES
D_LAT = MLA_HEADS * MLA_KV_LORA

_R_MQ, _R_DQ, _R_CQ, _R_CKV, _R_KPE = 0, D_MQ, D_MQ + D_DQK, D_MQ + D_DQK + MLA_Q_LORA, D_MQ + D_DQK + MLA_Q_LORA + MLA_KV_LORA
R_COLS = _R_KPE + LANES
_T_MK, _T_MV, _T_DK, _T_DV, _T_KPE = 0, D_MKV, 2 * D_MKV, 2 * D_MKV + D_DQK, 2 * D_MKV + D_DQK + D_DV
T_ROWS = _T_KPE + MLA_ROPE_DIM


def _rms(x, g, eps):
    return x * lax.rsqrt(jnp.mean(x * x, axis=-1, keepdims=True) + eps) * g


def _nt(a, b):
    return lax.dot_general(a, b, (((1,), (1,)), ((), ())), preferred_element_type=F32)


def _mm(a, b):
    return jnp.dot(a, b, preferred_element_type=F32)


def _rope_rows(x, cos, sin_signed, half):
    lane = lax.broadcasted_iota(jnp.int32, x.shape, 1)
    first = (lane & (2 * half - 1)) < half
    rot = jnp.where(first, pltpu.roll(x, LANES - half, axis=1), pltpu.roll(x, half, axis=1))
    return x * cos + rot * sin_signed


def _proj_kernel(x_ref, g_ref, wr_ref, wt_ref, gq_ref, wqn_ref, wqr_ref, wuk_ref, gkv_ref,
                 c64_ref, s64_ref, c32_ref, s32_ref, ct64_ref, st64_ref, ct32_ref, st32_ref,
                 *rest, layer):
    n_prev = len(rest) - 10
    (mq_ref, dq_ref, qm_ref, km_ref, ckv_ref,
     mkT_ref, mvT_ref, dkT_ref, dvT_ref, kpeT_ref) = rest[n_prev:]
    for prev_ref, out_ref in zip(rest[:n_prev], rest[n_prev + 4:]):
        out_ref[0:layer] = prev_ref[...]
    x = x_ref[0]
    h = _rms(x, g_ref[...], NORM_EPS).astype(BF16)
    zr = _mm(h, wr_ref[...])
    zt = _nt(wt_ref[...], h)

    c64, s64, c32, s32 = c64_ref[...], s64_ref[...], c32_ref[...], s32_ref[...]
    for c in range(D_MQ // LANES):
        mq_ref[0, :, c * LANES:(c + 1) * LANES] = _rope_rows(
            zr[:, _R_MQ + c * LANES:_R_MQ + (c + 1) * LANES], c64, s64, HEAD_DIM // 2)
    for c in range(D_DQK // LANES):
        dq_ref[0, :, c * LANES:(c + 1) * LANES] = _rope_rows(
            zr[:, _R_DQ + c * LANES:_R_DQ + (c + 1) * LANES], c32, s32, DIFF_QK_DIM // 2).astype(BF16)

    cqn = _rms(zr[:, _R_CQ:_R_CQ + MLA_Q_LORA], gq_ref[...], NORM_EPS).astype(BF16)
    qn = _mm(cqn, wqn_ref[...])
    qr = _mm(cqn, wqr_ref[...])
    for hd in range(MLA_HEADS):
        q_lat = _mm(qn[:, hd * MLA_NOPE_DIM:(hd + 1) * MLA_NOPE_DIM].astype(BF16), wuk_ref[hd])
        q_pe = _rope_rows(qr[:, hd * LANES:(hd + 1) * LANES], c32, s32, MLA_ROPE_DIM // 2)
        qm_ref[0, :, 2 * hd * LANES:(2 * hd + 1) * LANES] = q_lat.astype(BF16)
        qm_ref[0, :, (2 * hd + 1) * LANES:(2 * hd + 2) * LANES] = q_pe.astype(BF16)

    ckv = _rms(zr[:, _R_CKV:_R_CKV + MLA_KV_LORA], gkv_ref[...], NORM_EPS)
    ckv_ref[layer, 0] = ckv
    km_ref[0, :, 0:LANES] = ckv.astype(BF16)
    km_ref[0, :, LANES:2 * LANES] = _rope_rows(zr[:, _R_KPE:_R_KPE + LANES], c32, s32,
                                               MLA_ROPE_DIM // 2).astype(BF16)

    ct64, st64, ct32, st32 = ct64_ref[...], st64_ref[...], ct32_ref[...], st32_ref[...]

    def rope_cols(dst_ref, row0, n_groups, half, cos_t, sin_t):
        for gi in range(n_groups):
            r = row0 + gi * 2 * half
            x1, x2 = zt[r:r + half], zt[r + half:r + 2 * half]
            o = gi * 2 * half
            dst_ref[layer, 0, o:o + half, :] = x1 * cos_t - x2 * sin_t
            dst_ref[layer, 0, o + half:o + 2 * half, :] = x2 * cos_t + x1 * sin_t

    rope_cols(mkT_ref, _T_MK, MOBA_KV_HEADS, HEAD_DIM // 2, ct64, st64)
    mvT_ref[layer, 0] = zt[_T_MV:_T_MV + D_MKV]
    rope_cols(dkT_ref, _T_DK, DIFF_HEADS * 2, DIFF_QK_DIM // 2, ct32, st32)
    dvT_ref[layer, 0] = zt[_T_DV:_T_DV + D_DV]
    rope_cols(kpeT_ref, _T_KPE, 1, MLA_ROPE_DIM // 2, ct32, st32)


def _const_spec(shape):
    nd = len(shape)
    return pl.BlockSpec(shape, lambda *_: (0,) * nd)


def _projection(x, layer, wts, tabs, prev):
    B, S, D = x.shape
    tm = min(PROJ_ROWS, S)
    assert S % tm == 0
    grid = (B, S // tm)
    row_tab = pl.BlockSpec((tm, LANES), lambda b, i: (i, 0))
    in_specs = [
        pl.BlockSpec((1, tm, D), lambda b, i: (b, i, 0)),
        _const_spec((1, D)),
        _const_spec((D, R_COLS)), _const_spec((T_ROWS, D)),
        _const_spec((1, MLA_Q_LORA)),
        _const_spec((MLA_Q_LORA, MLA_HEADS * MLA_NOPE_DIM)),
        _const_spec((MLA_Q_LORA, MLA_HEADS * LANES)),
        _const_spec((MLA_HEADS, MLA_NOPE_DIM, MLA_KV_LORA)),
        _const_spec((1, MLA_KV_LORA)),
        row_tab, row_tab, row_tab, row_tab,
        pl.BlockSpec((HEAD_DIM // 2, tm), lambda b, i: (0, i)),
        pl.BlockSpec((HEAD_DIM // 2, tm), lambda b, i: (0, i)),
        pl.BlockSpec((DIFF_QK_DIM // 2, tm), lambda b, i: (0, i)),
        pl.BlockSpec((DIFF_QK_DIM // 2, tm), lambda b, i: (0, i)),
    ]
    args = [x, wts["g_attn"], wts["w_r"], wts["w_t"], wts["g_mla_q"], wts["w_uqn"], wts["w_uqr"],
            wts["w_uk"], wts["g_mla_kv"], *tabs]
    t_rows = (D_MKV, D_MKV, D_DQK, D_DV, MLA_ROPE_DIM)

    def stacked_specs(n):
        return ([pl.BlockSpec((n, 1, tm, MLA_KV_LORA), lambda b, i: (0, b, i, 0))]
                + [pl.BlockSpec((n, 1, r, tm), lambda b, i: (0, b, 0, i)) for r in t_rows])

    if prev is not None:
        in_specs += stacked_specs(layer)
        args += list(prev)
    n = layer + 1
    out_shape = [
        jax.ShapeDtypeStruct((B, S, D_MQ), F32),
        jax.ShapeDtypeStruct((B, S, D_DQK), BF16),
        jax.ShapeDtypeStruct((B, S, D_QM), BF16),
        jax.ShapeDtypeStruct((B, S, 2 * LANES), BF16),
        jax.ShapeDtypeStruct((n, B, S, MLA_KV_LORA), F32),
    ] + [jax.ShapeDtypeStruct((n, B, r, S), F32) for r in t_rows]
    out_specs = [
        pl.BlockSpec((1, tm, D_MQ), lambda b, i: (b, i, 0)),
        pl.BlockSpec((1, tm, D_DQK), lambda b, i: (b, i, 0)),
        pl.BlockSpec((1, tm, D_QM), lambda b, i: (b, i, 0)),
        pl.BlockSpec((1, tm, 2 * LANES), lambda b, i: (b, i, 0)),
    ] + stacked_specs(n)
    outs = pl.pallas_call(
        functools.partial(_proj_kernel, layer=layer),
        grid=grid, in_specs=in_specs, out_specs=out_specs, out_shape=out_shape,
        compiler_params=pltpu.CompilerParams(
            dimension_semantics=("arbitrary", "arbitrary"), vmem_limit_bytes=VMEM_LIMIT),
        name="projection",
    )(*args)
    return outs[:4], list(outs[4:])


def _causal_tile_mask(shape):
    return lax.broadcasted_iota(jnp.int32, shape, 1) <= lax.broadcasted_iota(jnp.int32, shape, 0)


def _attn_params():
    return pltpu.CompilerParams(
        dimension_semantics=("arbitrary", "arbitrary"), vmem_limit_bytes=VMEM_LIMIT)


def _flash_blocks(j, chains):
    state = []
    for own_scores, _, pv in chains:
        s = own_scores()
        m = jnp.max(s, axis=1, keepdims=True)
        p = jnp.exp(s - m)
        state.append((m, jnp.sum(p, axis=1, keepdims=True), pv(j, p.astype(BF16))))

    def body(n, carry):
        new = []
        for (m, l, acc), (_, past_scores, pv) in zip(carry, chains):
            s = past_scores(n)
            m_new = jnp.maximum(m, jnp.max(s, axis=1, keepdims=True))
            alpha = jnp.exp(m - m_new)
            p = jnp.exp(s - m_new)
            new.append((m_new, alpha * l + jnp.sum(p, axis=1, keepdims=True),
                        alpha * acc + pv(n, p.astype(BF16))))
        return tuple(new)

    state = lax.fori_loop(0, j, body, tuple(state))
    return [acc / l for _, l, acc in state]


def _moba_prompt_kernel(mq_ref, kT_ref, vT_ref, o_ref, kaug_ref, vdup_ref, kmean_ref, *, nb):
    j = pl.program_id(1)
    blk = MOBA_BLOCK
    lane = lax.broadcasted_iota(jnp.int32, (blk, LANES), 1)

    @pl.when(j == 0)
    def _():
        kt = kT_ref[0, 0]
        vt = vT_ref[0, 0]
        row = lax.broadcasted_iota(jnp.int32, (LANES, blk), 0)
        lane_k = lax.broadcasted_iota(jnp.int32, (D_MKV, LANES), 1)
        kmean = jnp.zeros((D_MKV, LANES), F32)
        for n in range(nb):
            kb = kt[:, n * blk:(n + 1) * blk]
            vb = vt[:, n * blk:(n + 1) * blk]
            ind = jnp.where(row == n, 1.0, 0.0).astype(BF16)
            for g in range(MOBA_KV_HEADS):
                kg = kb[g * HEAD_DIM:(g + 1) * HEAD_DIM].astype(BF16)
                vg = vb[g * HEAD_DIM:(g + 1) * HEAD_DIM].astype(BF16)
                kaug_ref[g * nb + n] = jnp.concatenate([kg, kg, ind], axis=0)
                vdup_ref[g * nb + n] = jnp.concatenate([vg, vg], axis=0)
            kmean = jnp.where(lane_k == n, jnp.mean(kb, axis=1, keepdims=True), kmean)
        kmean_ref[...] = kmean

    q = mq_ref[0]
    kmean = kmean_ref[...]
    lane_f = lane.astype(F32)
    valid = lane < j
    n_sel = min(MOBA_TOPK, nb - 1)
    scale = HEAD_DIM ** -0.5
    causal = _causal_tile_mask((blk, blk))
    chains = []
    for c in range(D_MQ // LANES):
        qc = q[:, c * LANES:(c + 1) * LANES]
        for hh in range(2):
            g = (2 * c + hh) // MOBA_GROUP
            in_half = (lane < HEAD_DIM) if hh == 0 else (lane >= HEAD_DIM)
            qh = jnp.where(in_half, qc, 0.0)
            kmg = kmean[g * HEAD_DIM:(g + 1) * HEAD_DIM]
            gate = jnp.dot(qh, jnp.concatenate([kmg, kmg], axis=0),
                           precision=lax.Precision.HIGHEST, preferred_element_type=F32)
            gate = jnp.where(valid, gate, -jnp.inf)
            bias = jnp.where(valid, NEG_INF, 0.0)
            for _ in range(n_sel):
                mx = jnp.max(gate, axis=1, keepdims=True)
                first = jnp.min(jnp.where(gate == mx, lane_f, float(LANES)), axis=1, keepdims=True)
                pick = lane_f == first
                bias = jnp.where(pick, 0.0, bias)
                gate = jnp.where(pick, -jnp.inf, gate)
            qa = jnp.concatenate([(qh * scale).astype(BF16), bias.astype(BF16)], axis=1)

            def own_scores(qa=qa, g=g):
                return jnp.where(causal, _mm(qa, kaug_ref[g * nb + j]), NEG_INF)

            def past_scores(n, qa=qa, g=g):
                return _mm(qa, kaug_ref[g * nb + n])

            def pv(n, p, g=g):
                return _nt(p, vdup_ref[g * nb + n])

            chains.append((own_scores, past_scores, pv))
    outs = _flash_blocks(j, chains)
    for c in range(D_MQ // LANES):
        o_ref[0, :, c * LANES:(c + 1) * LANES] = jnp.where(
            lane < HEAD_DIM, outs[2 * c], outs[2 * c + 1]).astype(BF16)


def _moba_prompt(mq, mkT, mvT, layer):
    B, S, _ = mq.shape
    assert S % MOBA_BLOCK == 0
    nb = S // MOBA_BLOCK
    assert nb <= LANES
    kv_spec = pl.BlockSpec((1, 1, D_MKV, S), lambda b, j: (layer, b, 0, 0))
    return pl.pallas_call(
        functools.partial(_moba_prompt_kernel, nb=nb),
        grid=(B, nb),
        in_specs=[pl.BlockSpec((1, MOBA_BLOCK, D_MQ), lambda b, j: (b, j, 0)), kv_spec, kv_spec],
        out_specs=pl.BlockSpec((1, MOBA_BLOCK, D_MQ), lambda b, j: (b, j, 0)),
        out_shape=jax.ShapeDtypeStruct((B, S, D_MQ), BF16),
        scratch_shapes=[
            pltpu.VMEM((MOBA_KV_HEADS * nb, 2 * LANES, MOBA_BLOCK), BF16),
            pltpu.VMEM((MOBA_KV_HEADS * nb, LANES, MOBA_BLOCK), BF16),
            pltpu.VMEM((D_MKV, LANES), F32),
        ],
        compiler_params=_attn_params(),
        name="moba_prompt",
    )(mq, mkT, mvT)


def _diff_lambda(lq1_ref, lk1_ref, lq2_ref, lk2_ref, lam_init):
    return (jnp.exp(jnp.sum(lq1_ref[...] * lk1_ref[...], axis=1, keepdims=True))
            - jnp.exp(jnp.sum(lq2_ref[...] * lk2_ref[...], axis=1, keepdims=True)) + lam_init)


def _diff_prompt_kernel(dq_ref, kT_ref, vT_ref, lq1_ref, lk1_ref, lq2_ref, lk2_ref, gsub_ref,
                        o_ref, kbf_ref, vbf_ref, *, nk, lam_init):
    j = pl.program_id(1)
    blk = ATTN_BLK

    @pl.when(j == 0)
    def _():
        for n in range(nk):
            kbf_ref[n] = kT_ref[0, 0, :, n * blk:(n + 1) * blk].astype(BF16)
            vbf_ref[n] = vT_ref[0, 0, :, n * blk:(n + 1) * blk].astype(BF16)

    lam = _diff_lambda(lq1_ref, lk1_ref, lq2_ref, lk2_ref, lam_init)
    q = dq_ref[0]
    lane = lax.broadcasted_iota(jnp.int32, (blk, LANES), 1)
    causal = _causal_tile_mask((blk, blk))
    scale = DIFF_QK_DIM ** -0.5
    zero = jnp.zeros((), BF16)
    chains = []
    for c in range(D_DQK // LANES):
        qc = q[:, c * LANES:(c + 1) * LANES]
        rows = slice(c * LANES, (c + 1) * LANES)
        for sub in range(4):
            lo = sub * DIFF_QK_DIM
            qm = jnp.where(jnp.logical_and(lane >= lo, lane < lo + DIFF_QK_DIM), qc, zero)

            def own_scores(qm=qm, rows=rows):
                return jnp.where(causal, _mm(qm, kbf_ref[j, rows, :]) * scale, NEG_INF)

            def past_scores(n, qm=qm, rows=rows):
                return _mm(qm, kbf_ref[n, rows, :]) * scale

            def pv(n, p, rows=rows):
                return _nt(p, vbf_ref[n, rows, :])

            chains.append((own_scores, past_scores, pv))
    maps = _flash_blocks(j, chains)
    for c in range(D_DQK // LANES):
        normed = []
        for hh in range(2):
            o_h = maps[4 * c + 2 * hh] - lam * maps[4 * c + 2 * hh + 1]
            in_half = (lane < DIFF_V_DIM) if hh == 0 else (lane >= DIFF_V_DIM)
            ms = jnp.sum(jnp.where(in_half, o_h * o_h, 0.0), axis=1, keepdims=True) * (1.0 / DIFF_V_DIM)
            normed.append(o_h * lax.rsqrt(ms + SUBLN_EPS))
        out = jnp.where(lane < DIFF_V_DIM, normed[0], normed[1]) * gsub_ref[...] * (1.0 - lam_init)
        o_ref[0, :, c * LANES:(c + 1) * LANES] = out.astype(BF16)


def _diff_prompt(dq, dkT, dvT, lams, gsub2, layer, lam_init):
    B, S, _ = dq.shape
    assert S % ATTN_BLK == 0
    nk = S // ATTN_BLK
    kv_spec = pl.BlockSpec((1, 1, D_DQK, S), lambda b, j: (layer, b, 0, 0))
    lam_spec = _const_spec((1, DIFF_QK_DIM))
    return pl.pallas_call(
        functools.partial(_diff_prompt_kernel, nk=nk, lam_init=lam_init),
        grid=(B, nk),
        in_specs=[pl.BlockSpec((1, ATTN_BLK, D_DQK), lambda b, j: (b, j, 0)), kv_spec, kv_spec,
                  lam_spec, lam_spec, lam_spec, lam_spec, _const_spec((1, LANES))],
        out_specs=pl.BlockSpec((1, ATTN_BLK, D_DV), lambda b, j: (b, j, 0)),
        out_shape=jax.ShapeDtypeStruct((B, S, D_DV), BF16),
        scratch_shapes=[pltpu.VMEM((nk, D_DQK, ATTN_BLK), BF16), pltpu.VMEM((nk, D_DV, ATTN_BLK), BF16)],
        compiler_params=_attn_params(),
        name="diff_prompt",
    )(dq, dkT, dvT, *lams, gsub2)


def _mla_prompt_kernel(qm_ref, km_ref, o_ref):
    j = pl.program_id(1)
    blk = ATTN_BLK
    causal = _causal_tile_mask((blk, blk))
    scale = (MLA_NOPE_DIM + MLA_ROPE_DIM) ** -0.5
    q = qm_ref[0]

    def keys(n):
        return km_ref[0, pl.ds(pl.multiple_of(n * blk, blk), blk), :]

    def pv(n, p):
        return _mm(p, km_ref[0, pl.ds(pl.multiple_of(n * blk, blk), blk), 0:MLA_KV_LORA])

    chains = []
    for hd in range(MLA_HEADS):
        qh = q[:, hd * 2 * LANES:(hd + 1) * 2 * LANES]

        def own_scores(qh=qh):
            return jnp.where(causal, _nt(qh, keys(j)) * scale, NEG_INF)

        def past_scores(n, qh=qh):
            return _nt(qh, keys(n)) * scale

        chains.append((own_scores, past_scores, pv))
    for hd, o in enumerate(_flash_blocks(j, chains)):
        o_ref[0, :, hd * MLA_KV_LORA:(hd + 1) * MLA_KV_LORA] = o.astype(BF16)


def _mla_prompt(qm, km):
    B, S, _ = qm.shape
    assert S % ATTN_BLK == 0
    return pl.pallas_call(
        _mla_prompt_kernel,
        grid=(B, S // ATTN_BLK),
        in_specs=[pl.BlockSpec((1, ATTN_BLK, D_QM), lambda b, j: (b, j, 0)),
                  pl.BlockSpec((1, S, 2 * LANES), lambda b, j: (b, 0, 0))],
        out_specs=pl.BlockSpec((1, ATTN_BLK, D_LAT), lambda b, j: (b, j, 0)),
        out_shape=jax.ShapeDtypeStruct((B, S, D_LAT), BF16),
        compiler_params=_attn_params(),
        name="mla_prompt",
    )(qm, km)


def _mix_mlp_kernel(x_ref, om_ref, od_ref, ol_ref, wuv_ref, wo_ref, gm_ref, wup_ref, wdn_ref, gf_ref,
                    o_ref, *, final):
    attn = _mm(om_ref[...], wo_ref[0:D_MQ, :]) + _mm(od_ref[...], wo_ref[D_MQ:D_MQ + D_DV, :])
    ol = ol_ref[...]
    for c in range(MLA_HEADS // 2):
        o_mla = (_mm(ol[:, 2 * c * LANES:(2 * c + 1) * LANES], wuv_ref[2 * c])
                 + _mm(ol[:, (2 * c + 1) * LANES:(2 * c + 2) * LANES], wuv_ref[2 * c + 1]))
        r0 = D_MQ + D_DV + c * LANES
        attn = attn + _mm(o_mla.astype(BF16), wo_ref[r0:r0 + LANES, :])
    x1 = x_ref[...] + attn
    h = _rms(x1, gm_ref[...], NORM_EPS).astype(BF16)
    d_ff = wup_ref.shape[1]
    down = jnp.zeros_like(x1)
    for c in range(d_ff // FF_CHUNK):
        up = jnp.maximum(_mm(h, wup_ref[:, c * FF_CHUNK:(c + 1) * FF_CHUNK]), 0.0)
        down = down + _mm((up * up).astype(BF16), wdn_ref[c * FF_CHUNK:(c + 1) * FF_CHUNK, :])
    x2 = x1 + down
    o_ref[...] = _rms(x2, gf_ref[...], NORM_EPS) if final else x2


def _mix_mlp(x, om, od, ol, wts, g_final, final):
    N, D = x.shape
    tm = min(PROJ_ROWS, N)
    assert N % tm == 0
    d_ff = wts["w_up"].shape[1]
    assert d_ff % FF_CHUNK == 0

    def resident(shape):
        nd = len(shape)
        return pl.BlockSpec(shape, lambda i: (0,) * nd, pipeline_mode=pl.Buffered(1))

    def rows(width):
        return pl.BlockSpec((tm, width), lambda i: (i, 0))

    return pl.pallas_call(
        functools.partial(_mix_mlp_kernel, final=final),
        grid=(N // tm,),
        in_specs=[rows(D), rows(D_MQ), rows(D_DV), rows(D_LAT),
                  resident((MLA_HEADS, MLA_KV_LORA, LANES)), resident((D, D)), resident((1, D)),
                  resident((D, d_ff)), resident((d_ff, D)), resident((1, D))],
        out_specs=rows(D),
        out_shape=jax.ShapeDtypeStruct((N, D), F32),
        compiler_params=pltpu.CompilerParams(
            dimension_semantics=("arbitrary",), vmem_limit_bytes=VMEM_LIMIT),
        name="mix_mlp",
    )(x, om, od, ol, wts["w_uv"], wts["w_out"], wts["g_mlp"], wts["w_up"], wts["w_down"], g_final)


def _page_window(buf, slot, p, pos_major):
    t, r = divmod(p, PAGES_PER_TILE)
    if pos_major:
        return buf.at[slot, t, pl.ds(r * LANES, LANES), :]
    return buf.at[slot, t, :, pl.ds(r * LANES, LANES)]


def _page_copies(pt_ref, seq, slot, pools, bufs, pos_major, sems, layer, n_pages):
    copies = []
    for k, (pool, buf, pm) in enumerate(zip(pools, bufs, pos_major)):
        for p in range(n_pages):
            copies.append(pltpu.make_async_copy(
                pool.at[layer, pt_ref[seq, p]], _page_window(buf, slot, p, pm), sems.at[k, slot]))
    return copies


def _fetch_pages(pt_ref, pools, bufs, pos_major, news, sems, layer, n_pages):
    b = pl.program_id(0)
    slot = lax.rem(b, 2)

    @pl.when(b == 0)
    def _():
        for cp in _page_copies(pt_ref, 0, 0, pools, bufs, pos_major, sems, layer, n_pages):
            cp.start()

    @pl.when(b + 1 < pl.num_programs(0))
    def _():
        for cp in _page_copies(pt_ref, b + 1, 1 - slot, pools, bufs, pos_major, sems, layer, n_pages):
            cp.start()

    for cp in _page_copies(pt_ref, b, slot, pools, bufs, pos_major, sems, layer, n_pages):
        cp.wait()
    last = n_pages // PAGES_PER_TILE
    for buf, new_ref, pm in zip(bufs, news, pos_major):
        new = new_ref[0]
        if pm:
            buf[slot, last, 0:LANES, :] = new
            buf[slot, last, LANES:KEY_TILE, :] = jnp.zeros((KEY_TILE - LANES, new.shape[1]), F32)
        else:
            buf[slot, last, :, 0:LANES] = new
            buf[slot, last, :, LANES:KEY_TILE] = jnp.zeros((new.shape[0], KEY_TILE - LANES), F32)
    return slot


def _mask_new_tokens(s_ref, last, t_new):
    assert t_new & (t_new - 1) == 0
    n_rows = s_ref.shape[1]
    row = lax.broadcasted_iota(jnp.int32, (n_rows, KEY_TILE), 0)
    lane = lax.broadcasted_iota(jnp.int32, (n_rows, KEY_TILE), 1)
    s_ref[last] = jnp.where(lane <= (row & (t_new - 1)), s_ref[last], NEG_INF)


def _softmax_tiles(s_ref):
    s = s_ref[...]
    m = jnp.max(jnp.max(s, axis=0), axis=1, keepdims=True)
    p = jnp.exp(s - m[None])
    l = jnp.sum(jnp.sum(p, axis=0), axis=1, keepdims=True)
    s_ref[...] = p / l[None]


def _tile_loop(n, body, init):
    return lax.fori_loop(0, n, body, init, unroll=TILE_UNROLL)


def _moba_sample_kernel(pt_ref, q_ref, kn_ref, vn_ref, kpool, vpool, o_ref, kbuf, vbuf, sems, s_ref,
                        *, layer, n_pages, t_new):
    slot = _fetch_pages(pt_ref, (kpool, vpool), (kbuf, vbuf), (False, False), (kn_ref, vn_ref),
                        sems, layer, n_pages)
    n_blocks = n_pages // PAGES_PER_TILE
    q = q_ref[0]
    n_rows = q.shape[0]
    qb = (q * HEAD_DIM ** -0.5).astype(BF16)

    def scores(t, _):
        s_ref[t] = _mm(qb, kbuf[slot, t].astype(BF16))
        return 0

    _tile_loop(n_blocks + 1, scores, 0)

    lane_k = lax.broadcasted_iota(jnp.int32, (D_MKV, LANES), 1)

    def block_mean(n, kmean):
        mean = jnp.sum(kbuf[slot, n], axis=1, keepdims=True) * (1.0 / MOBA_BLOCK)
        return jnp.where(lane_k == n, mean, kmean)

    kmean = _tile_loop(n_blocks, block_mean, jnp.zeros((D_MKV, LANES), F32))
    gate = jnp.dot(q, kmean, precision=lax.Precision.HIGHEST, preferred_element_type=F32)
    lane = lax.broadcasted_iota(jnp.int32, (n_rows, LANES), 1)
    lane_f = lane.astype(F32)
    valid = lane < n_blocks
    gate = jnp.where(valid, gate, -jnp.inf)
    sel = jnp.zeros((n_rows, LANES), F32)
    for _ in range(min(MOBA_TOPK, n_blocks)):
        mx = jnp.max(gate, axis=1, keepdims=True)
        first = jnp.min(jnp.where(gate == mx, lane_f, float(LANES)), axis=1, keepdims=True)
        pick = lane_f == first
        sel = jnp.where(jnp.logical_and(pick, valid), 1.0, sel)
        gate = jnp.where(pick, -jnp.inf, gate)

    def mask_block(n, _):
        chosen = jnp.sum(jnp.where(lane == n, sel, 0.0), axis=1, keepdims=True) > 0.5
        s_ref[n] = jnp.where(chosen, s_ref[n], NEG_INF)
        return 0

    _tile_loop(n_blocks, mask_block, 0)
    _mask_new_tokens(s_ref, n_blocks, t_new)
    _softmax_tiles(s_ref)

    def pv(t, acc):
        return acc + _nt(s_ref[t].astype(BF16), vbuf[slot, t].astype(BF16))

    o_ref[0] = _tile_loop(n_blocks + 1, pv, jnp.zeros((n_rows, D_MKV), F32))


def _diff_sample_kernel(pt_ref, q_ref, kn_ref, vn_ref, lq1_ref, lk1_ref, lq2_ref, lk2_ref, gsub_ref,
                        kpool, vpool, o_ref, kbuf, vbuf, sems, s_ref,
                        *, layer, n_pages, t_new, lam_init):
    slot = _fetch_pages(pt_ref, (kpool, vpool), (kbuf, vbuf), (False, False), (kn_ref, vn_ref),
                        sems, layer, n_pages)
    n_tiles = n_pages // PAGES_PER_TILE + 1
    q = q_ref[0]
    n_rows = q.shape[0]
    scale = DIFF_QK_DIM ** -0.5

    def scores(t, _):
        s_ref[t] = _mm(q, kbuf[slot, t].astype(BF16)) * scale
        return 0

    _tile_loop(n_tiles, scores, 0)
    _mask_new_tokens(s_ref, n_tiles - 1, t_new)
    _softmax_tiles(s_ref)
    lam = _diff_lambda(lq1_ref, lk1_ref, lq2_ref, lk2_ref, lam_init)
    half = n_rows // 2

    def pv(t, acc):
        a = (s_ref[t, 0:half, :] - lam * s_ref[t, half:n_rows, :]).astype(BF16)
        return acc + _nt(a, vbuf[slot, t].astype(BF16))

    o = _tile_loop(n_tiles, pv, jnp.zeros((half, D_DV), F32))
    assert t_new & (t_new - 1) == 0
    row = lax.broadcasted_iota(jnp.int32, o.shape, 0)
    lane = lax.broadcasted_iota(jnp.int32, o.shape, 1)
    own = (row >> (t_new.bit_length() - 1)) == (lane >> (DIFF_V_DIM.bit_length() - 1))
    ms = jnp.sum(jnp.where(own, o * o, 0.0), axis=1, keepdims=True) * (1.0 / DIFF_V_DIM)
    o_ref[0] = jnp.where(own, o * lax.rsqrt(ms + SUBLN_EPS), 0.0) * gsub_ref[...] * (1.0 - lam_init)


def _mla_sample_kernel(pt_ref, ql_ref, qp_ref, cn_ref, pn_ref, cpool, ppool, o_ref, cbuf, pbuf, sems, s_ref,
                       *, layer, n_pages, t_new):
    slot = _fetch_pages(pt_ref, (cpool, ppool), (cbuf, pbuf), (True, False), (cn_ref, pn_ref),
                        sems, layer, n_pages)
    n_tiles = n_pages // PAGES_PER_TILE + 1
    ql, qp = ql_ref[0], qp_ref[0]
    n_rows = ql.shape[0]
    scale = (MLA_NOPE_DIM + MLA_ROPE_DIM) ** -0.5

    def scores(t, _):
        s_ref[t] = (_nt(ql, cbuf[slot, t].astype(BF16)) + _mm(qp, pbuf[slot, t].astype(BF16))) * scale
        return 0

    _tile_loop(n_tiles, scores, 0)
    _mask_new_tokens(s_ref, n_tiles - 1, t_new)
    _softmax_tiles(s_ref)

    def pv(t, acc):
        return acc + _mm(s_ref[t].astype(BF16), cbuf[slot, t].astype(BF16))

    o_ref[0] = _tile_loop(n_tiles, pv, jnp.zeros((n_rows, MLA_KV_LORA), F32))


def _paged_call(kernel, page_table, per_seq, consts, pools, n_rows, out_rows, out_lanes, tile_shapes, name):
    Bd, n_pages = page_table.shape
    assert n_pages % PAGES_PER_TILE == 0
    n_tiles = n_pages // PAGES_PER_TILE + 1
    in_specs = [pl.BlockSpec((1,) + a.shape[1:], lambda b, pt: (b, 0, 0)) for a in per_seq]
    in_specs += [pl.BlockSpec(a.shape, lambda b, pt: (0, 0)) for a in consts]
    in_specs += [pl.BlockSpec(memory_space=pl.ANY) for _ in pools]
    scratch = [pltpu.VMEM((2, n_tiles) + ts, F32) for ts in tile_shapes]
    scratch.append(pltpu.SemaphoreType.DMA((len(pools), 2)))
    scratch.append(pltpu.VMEM((n_tiles, n_rows, KEY_TILE), F32))
    return pl.pallas_call(
        kernel,
        grid_spec=pltpu.PrefetchScalarGridSpec(
            num_scalar_prefetch=1, grid=(Bd,), in_specs=in_specs,
            out_specs=pl.BlockSpec((1, out_rows, out_lanes), lambda b, pt: (b, 0, 0)),
            scratch_shapes=scratch),
        out_shape=jax.ShapeDtypeStruct((Bd, out_rows, out_lanes), F32),
        compiler_params=pltpu.CompilerParams(
            dimension_semantics=("arbitrary",), vmem_limit_bytes=VMEM_LIMIT),
        name=name,
    )(page_table, *per_seq, *consts, *pools)


def _pad_lanes(a):
    return jnp.pad(a, ((0, 0), (0, 0), (0, LANES - a.shape[2])))


def _sample_attention(acts, new_rows, layer, Bd, T, page_table, pools, lams, gsub, lam_init):
    mq, dq, qm, _ = acts
    ckv_n, mkT, mvT, dkT, dvT, kpeT = new_rows
    n_pages = page_table.shape[1]
    assert MOBA_BLOCK == KEY_TILE
    kw = dict(layer=layer, n_pages=n_pages, t_new=T)

    def per_seq_cols(a):
        f = a.shape[1]
        return _pad_lanes(a[0].reshape(f, Bd, T).transpose(1, 0, 2))

    q = mq[0].reshape(Bd, T, MOBA_HEADS, HEAD_DIM).transpose(0, 2, 1, 3)
    z = jnp.zeros_like(q[:, :MOBA_GROUP])
    q_moba = jnp.concatenate([jnp.concatenate([q[:, :MOBA_GROUP], z], axis=-1),
                              jnp.concatenate([z, q[:, MOBA_GROUP:]], axis=-1)], axis=1)
    q_moba = q_moba.reshape(Bd, MOBA_HEADS * T, LANES)
    o = _paged_call(
        functools.partial(_moba_sample_kernel, **kw), page_table,
        [q_moba, per_seq_cols(mkT[layer]), per_seq_cols(mvT[layer])], [], pools[0:2],
        MOBA_HEADS * T, MOBA_HEADS * T, LANES, [(D_MKV, KEY_TILE), (D_MKV, KEY_TILE)], "moba_sample")
    o = o.reshape(Bd, MOBA_HEADS, T, MOBA_KV_HEADS, HEAD_DIM)
    om = jnp.stack([o[:, h, :, h // MOBA_GROUP] for h in range(MOBA_HEADS)], axis=2)
    om = om.reshape(Bd * T, D_MQ).astype(BF16)

    q = dq[0].reshape(Bd, T, DIFF_HEADS, 2, DIFF_QK_DIM)
    q_diff = jnp.zeros((Bd, 2, DIFF_HEADS, T, D_DQK), BF16)
    for h in range(DIFF_HEADS):
        for c in range(2):
            lo = (2 * h + c) * DIFF_QK_DIM
            q_diff = q_diff.at[:, c, h, :, lo:lo + DIFF_QK_DIM].set(q[:, :, h, c])
    q_diff = q_diff.reshape(Bd, 2 * DIFF_HEADS * T, D_DQK)
    o = _paged_call(
        functools.partial(_diff_sample_kernel, lam_init=lam_init, **kw), page_table,
        [q_diff, per_seq_cols(dkT[layer]), per_seq_cols(dvT[layer])], [*lams, gsub], pools[2:4],
        2 * DIFF_HEADS * T, DIFF_HEADS * T, D_DV, [(D_DQK, KEY_TILE), (D_DV, KEY_TILE)], "diff_sample")
    o = o.reshape(Bd, DIFF_HEADS, T, DIFF_HEADS, DIFF_V_DIM)
    od = jnp.stack([o[:, h, :, h] for h in range(DIFF_HEADS)], axis=2).reshape(Bd * T, D_DV).astype(BF16)

    q = qm[0].reshape(Bd, T, MLA_HEADS, 2 * LANES).transpose(0, 2, 1, 3).reshape(Bd, MLA_HEADS * T, 2 * LANES)
    ckv_new = jnp.pad(ckv_n[layer, 0].reshape(Bd, T, MLA_KV_LORA), ((0, 0), (0, LANES - T), (0, 0)))
    o = _paged_call(
        functools.partial(_mla_sample_kernel, **kw), page_table,
        [q[:, :, :MLA_KV_LORA], q[:, :, LANES:LANES + MLA_ROPE_DIM], ckv_new, per_seq_cols(kpeT[layer])],
        [], pools[4:6], MLA_HEADS * T, MLA_HEADS * T, MLA_KV_LORA,
        [(KEY_TILE, MLA_KV_LORA), (MLA_ROPE_DIM, KEY_TILE)], "mla_sample")
    ol = o.reshape(Bd, MLA_HEADS, T, MLA_KV_LORA).transpose(0, 2, 1, 3).reshape(Bd * T, D_LAT).astype(BF16)
    return om, od, ol


def _rope_tables(pos):
    out_rows, out_cols = [], []
    for d in (HEAD_DIM, DIFF_QK_DIM):
        half = d // 2
        inv = 1.0 / (ROPE_THETA ** (jnp.arange(half, dtype=F32) * (2.0 / d)))
        ang = pos.astype(F32)[:, None] * inv[None, :]
        cos, sin = jnp.cos(ang), jnp.sin(ang)
        out_rows += [jnp.tile(cos, (1, LANES // half)),
                     jnp.tile(jnp.concatenate([-sin, sin], axis=1), (1, LANES // d))]
        out_cols += [cos.T, sin.T]
    return out_rows + out_cols


def _layer_weights(l, g_attn, w_in, g_mla_q, w_mla_uq, g_mla_kv, w_mla_uk, w_mla_uv, w_out, g_mlp,
                   w_up, w_down):
    w = w_in[l]
    d = w.shape[0]
    w_r = jnp.concatenate([w[:, _O_MQ:_O_MK], w[:, _O_DQ:_O_DK], w[:, _O_CQ:_O_KPE], w[:, _O_KPE:_O_END],
                           jnp.zeros((d, LANES - MLA_ROPE_DIM), w.dtype)], axis=1)
    w_t = jnp.concatenate([w[:, _O_MK:_O_DQ], w[:, _O_DK:_O_CQ], w[:, _O_KPE:_O_END]], axis=1).T
    uq = w_mla_uq[l].reshape(MLA_Q_LORA, MLA_HEADS, MLA_NOPE_DIM + MLA_ROPE_DIM)
    w_uqn = uq[:, :, :MLA_NOPE_DIM].reshape(MLA_Q_LORA, MLA_HEADS * MLA_NOPE_DIM)
    w_uqr = jnp.pad(uq[:, :, MLA_NOPE_DIM:], ((0, 0), (0, 0), (0, LANES - MLA_ROPE_DIM)))
    w_uqr = w_uqr.reshape(MLA_Q_LORA, MLA_HEADS * LANES)
    w_uk = w_mla_uk[l].transpose(1, 2, 0)
    uv = w_mla_uv[l].transpose(1, 0, 2)
    w_uv = jnp.stack([jnp.pad(uv[h], ((0, 0), ((h % 2) * MLA_V_DIM, LANES - MLA_V_DIM - (h % 2) * MLA_V_DIM)))
                      for h in range(MLA_HEADS)], axis=0)
    return {
        "g_attn": g_attn[l][None], "w_r": w_r.astype(BF16), "w_t": w_t.astype(BF16),
        "g_mla_q": g_mla_q[l][None], "w_uqn": w_uqn.astype(BF16), "w_uqr": w_uqr.astype(BF16),
        "w_uk": w_uk.astype(BF16), "g_mla_kv": g_mla_kv[l][None], "w_uv": w_uv.astype(BF16),
        "w_out": w_out[l].astype(BF16), "g_mlp": g_mlp[l][None],
        "w_up": w_up[l].astype(BF16), "w_down": w_down[l].astype(BF16),
    }


def _feature_major_pool(cache):
    nd = cache.ndim
    perm = (0, 1) + tuple(range(3, nd)) + (2,)
    t = cache.transpose(perm)
    return t.reshape(t.shape[0], t.shape[1], -1, t.shape[-1])


def kernel(x_prompt, x_sample, cache_moba_k, cache_moba_v, cache_diff_k, cache_diff_v, cache_mla_ckv, cache_mla_kpe, page_table, g_attn, w_in, g_mla_q, w_mla_uq, g_mla_kv, w_mla_uk, w_mla_uv, lambda_q1, lambda_k1, lambda_q2, lambda_k2, g_diff_sub, w_out, g_mlp, w_up, w_down, g_final):
    B, S, D = x_prompt.shape
    Bd, T, _ = x_sample.shape
    depth = w_in.shape[0]
    n_pages = page_table.shape[1]
    past_len = n_pages * cache_moba_k.shape[2]
    assert cache_moba_k.shape[2] == LANES
    assert past_len % MOBA_BLOCK == 0

    tabs_p = _rope_tables(jnp.arange(S))
    tabs_s = _rope_tables(past_len + (jnp.arange(Bd * T) % T))
    pools = [_feature_major_pool(cache_moba_k), _feature_major_pool(cache_moba_v),
             _feature_major_pool(cache_diff_k), _feature_major_pool(cache_diff_v),
             cache_mla_ckv, _feature_major_pool(cache_mla_kpe)]
    gf = g_final[None]

    xp = x_prompt
    xs = x_sample.reshape(1, Bd * T, D)
    rows_p = rows_s = None
    for l in range(depth):
        final = l == depth - 1
        lam_init = 0.8 - 0.6 * math.exp(-0.3 * l)
        wts = _layer_weights(l, g_attn, w_in, g_mla_q, w_mla_uq, g_mla_kv, w_mla_uk, w_mla_uv, w_out,
                             g_mlp, w_up, w_down)
        lams = [lambda_q1[l][None], lambda_k1[l][None], lambda_q2[l][None], lambda_k2[l][None]]
        gsub = g_diff_sub[l]
        gsub2 = jnp.tile(gsub, LANES // DIFF_V_DIM)[None]
        gsub4 = jnp.tile(gsub, DIFF_HEADS)[None]

        (mq, dq, qm, km), rows_p = _projection(xp, l, wts, tabs_p, rows_p)
        om = _moba_prompt(mq, rows_p[1], rows_p[2], l)
        od = _diff_prompt(dq, rows_p[3], rows_p[4], lams, gsub2, l, lam_init)
        ol = _mla_prompt(qm, km)
        xp = _mix_mlp(xp.reshape(B * S, D), om.reshape(B * S, D_MQ), od.reshape(B * S, D_DV),
                      ol.reshape(B * S, D_LAT), wts, gf, final).reshape(B, S, D)

        acts, rows_s = _projection(xs, l, wts, tabs_s, rows_s)
        om, od, ol = _sample_attention(acts, rows_s, l, Bd, T, page_table, pools, lams, gsub4, lam_init)
        xs = _mix_mlp(xs.reshape(Bd * T, D), om, od, ol, wts, gf, final).reshape(1, Bd * T, D)

    def finish(rows, b, s):
        ckv, mkT, mvT, dkT, dvT, kpeT = rows
        mk = mkT.reshape(depth, b, MOBA_KV_HEADS, HEAD_DIM, s).transpose(0, 1, 4, 2, 3)
        mv = mvT.reshape(depth, b, MOBA_KV_HEADS, HEAD_DIM, s).transpose(0, 1, 4, 2, 3)
        dk = dkT.reshape(depth, b, DIFF_HEADS, 2, DIFF_QK_DIM, s).transpose(0, 1, 5, 2, 3, 4)
        dv = dvT.reshape(depth, b, DIFF_HEADS, DIFF_V_DIM, s).transpose(0, 1, 4, 2, 3)
        return mk, mv, dk, dv, ckv, kpeT.transpose(0, 1, 3, 2)

    outs_p = finish(rows_p, B, S)
    outs_s = tuple(a.reshape((depth, Bd, T) + a.shape[3:]) for a in finish(rows_s, 1, Bd * T))
    return (xp, xs.reshape(Bd, T, D)) + outs_p + outs_s
```

```python
import functools
import math

import jax
import jax.numpy as jnp
from jax import lax
from jax.experimental import pallas as pl
from jax.experimental.pallas import tpu as pltpu

F32 = jnp.float32
BF16 = jnp.bfloat16

HEAD_DIM = 64
MOBA_HEADS = 6
MOBA_KV_HEADS = 2
MOBA_GROUP = MOBA_HEADS // MOBA_KV_HEADS
MOBA_BLOCK = 256
MOBA_TOPK = 3
DIFF_HEADS = 4
DIFF_QK_DIM = 32
DIFF_V_DIM = 2 * DIFF_QK_DIM
MLA_HEADS = 6
MLA_Q_LORA = 256
MLA_KV_LORA = 128
MLA_NOPE_DIM = 64
MLA_ROPE_DIM = 32
MLA_V_DIM = 64
ROPE_THETA = 10000.0
NORM_EPS = 1e-6
SUBLN_EPS = 1e-5
NEG_INF = -1e30

LANES = 128
ATTN_BLK = 256
PROJ_ROWS = 512
FF_CHUNK = 1024
PAGES_PER_TILE = 2
KEY_TILE = PAGES_PER_TILE * LANES
TILE_UNROLL = 8
VMEM_LIMIT = 48 * 1024 * 1024

_IN_SIZES = (MOBA_HEADS * HEAD_DIM, MOBA_KV_HEADS * HEAD_DIM, MOBA_KV_HEADS * HEAD_DIM,
             DIFF_HEADS * 2 * DIFF_QK_DIM, DIFF_HEADS * 2 * DIFF_QK_DIM, DIFF_HEADS * DIFF_V_DIM,
             MLA_Q_LORA, MLA_KV_LORA, MLA_ROPE_DIM)
_OFF = [0]
for _s in _IN_SIZES:
    _OFF.append(_OFF[-1] + _s)
(_O_MQ, _O_MK, _O_MV, _O_DQ, _O_DK, _O_DV, _O_CQ, _O_CKV, _O_KPE, _O_END) = _OFF

D_MQ = MOBA_HEADS * HEAD_DIM
D_MKV = MOBA_KV_HEADS * HEAD_DIM
D_DQK = DIFF_HEADS * 2 * DIFF_QK_DIM
D_DV = DIFF_HEADS * DIFF_V_DIM
D_QM = MLA_HEADS * 2 * LANES
D_LAT = MLA_HEADS * MLA_KV_LORA

_R_MQ, _R_DQ, _R_CQ, _R_CKV, _R_KPE = 0, D_MQ, D_MQ + D_DQK, D_MQ + D_DQK + MLA_Q_LORA, D_MQ + D_DQK + MLA_Q_LORA + MLA_KV_LORA
R_COLS = _R_KPE + LANES
_T_MK, _T_MV, _T_DK, _T_DV, _T_KPE = 0, D_MKV, 2 * D_MKV, 2 * D_MKV + D_DQK, 2 * D_MKV + D_DQK + D_DV
T_ROWS = _T_KPE + MLA_ROPE_DIM


def _rms(x, g, eps):
    return x * lax.rsqrt(jnp.mean(x * x, axis=-1, keepdims=True) + eps) * g


def _nt(a, b):
    return lax.dot_general(a, b, (((1,), (1,)), ((), ())), preferred_element_type=F32)


def _mm(a, b):
    return jnp.dot(a, b, preferred_element_type=F32)


def _rope_rows(x, cos, sin_signed, half):
    lane = lax.broadcasted_iota(jnp.int32, x.shape, 1)
    first = (lane & (2 * half - 1)) < half
    rot = jnp.where(first, pltpu.roll(x, LANES - half, axis=1), pltpu.roll(x, half, axis=1))
    return x * cos + rot * sin_signed


def _proj_kernel(x_ref, g_ref, wr_ref, wt_ref, gq_ref, wqn_ref, wqr_ref, wuk_ref, gkv_ref,
                 c64_ref, s64_ref, c32_ref, s32_ref, ct64_ref, st64_ref, ct32_ref, st32_ref,
                 *rest, layer):
    n_prev = len(rest) - 10
    (mq_ref, dq_ref, qm_ref, km_ref, ckv_ref,
     mkT_ref, mvT_ref, dkT_ref, dvT_ref, kpeT_ref) = rest[n_prev:]
    for prev_ref, out_ref in zip(rest[:n_prev], rest[n_prev + 4:]):
        out_ref[0:layer] = prev_ref[...]
    x = x_ref[0]
    h = _rms(x, g_ref[...], NORM_EPS).astype(BF16)
    zr = _mm(h, wr_ref[...])
    zt = _nt(wt_ref[...], h)

    c64, s64, c32, s32 = c64_ref[...], s64_ref[...], c32_ref[...], s32_ref[...]
    for c in range(D_MQ // LANES):
        mq_ref[0, :, c * LANES:(c + 1) * LANES] = _rope_rows(
            zr[:, _R_MQ + c * LANES:_R_MQ + (c + 1) * LANES], c64, s64, HEAD_DIM // 2)
    for c in range(D_DQK // LANES):
        dq_ref[0, :, c * LANES:(c + 1) * LANES] = _rope_rows(
            zr[:, _R_DQ + c * LANES:_R_DQ + (c + 1) * LANES], c32, s32, DIFF_QK_DIM // 2).astype(BF16)

    cqn = _rms(zr[:, _R_CQ:_R_CQ + MLA_Q_LORA], gq_ref[...], NORM_EPS).astype(BF16)
    qn = _mm(cqn, wqn_ref[...])
    qr = _mm(cqn, wqr_ref[...])
    for hd in range(MLA_HEADS):
        q_lat = _mm(qn[:, hd * MLA_NOPE_DIM:(hd + 1) * MLA_NOPE_DIM].astype(BF16), wuk_ref[hd])
        q_pe = _rope_rows(qr[:, hd * LANES:(hd + 1) * LANES], c32, s32, MLA_ROPE_DIM // 2)
        qm_ref[0, :, 2 * hd * LANES:(2 * hd + 1) * LANES] = q_lat.astype(BF16)
        qm_ref[0, :, (2 * hd + 1) * LANES:(2 * hd + 2) * LANES] = q_pe.astype(BF16)

    ckv = _rms(zr[:, _R_CKV:_R_CKV + MLA_KV_LORA], gkv_ref[...], NORM_EPS)
    ckv_ref[layer, 0] = ckv
    km_ref[0, :, 0:LANES] = ckv.astype(BF16)
    km_ref[0, :, LANES:2 * LANES] = _rope_rows(zr[:, _R_KPE:_R_KPE + LANES], c32, s32,
                                               MLA_ROPE_DIM // 2).astype(BF16)

    ct64, st64, ct32, st32 = ct64_ref[...], st64_ref[...], ct32_ref[...], st32_ref[...]

    def rope_cols(dst_ref, row0, n_groups, half, cos_t, sin_t):
        for gi in range(n_groups):
            r = row0 + gi * 2 * half
            x1, x2 = zt[r:r + half], zt[r + half:r + 2 * half]
            o = gi * 2 * half
            dst_ref[layer, 0, o:o + half, :] = x1 * cos_t - x2 * sin_t
            dst_ref[layer, 0, o + half:o + 2 * half, :] = x2 * cos_t + x1 * sin_t

    rope_cols(mkT_ref, _T_MK, MOBA_KV_HEADS, HEAD_DIM // 2, ct64, st64)
    mvT_ref[layer, 0] = zt[_T_MV:_T_MV + D_MKV]
    rope_cols(dkT_ref, _T_DK, DIFF_HEADS * 2, DIFF_QK_DIM // 2, ct32, st32)
    dvT_ref[layer, 0] = zt[_T_DV:_T_DV + D_DV]
    rope_cols(kpeT_ref, _T_KPE, 1, MLA_ROPE_DIM // 2, ct32, st32)


def _const_spec(shape):
    nd = len(shape)
    return pl.BlockSpec(shape, lambda *_: (0,) * nd)


def _projection(x, layer, wts, tabs, prev):
    B, S, D = x.shape
    tm = min(PROJ_ROWS, S)
    assert S % tm == 0
    grid = (B, S // tm)
    row_tab = pl.BlockSpec((tm, LANES), lambda b, i: (i, 0))
    in_specs = [
        pl.BlockSpec((1, tm, D), lambda b, i: (b, i, 0)),
        _const_spec((1, D)),
        _const_spec((D, R_COLS)), _const_spec((T_ROWS, D)),
        _const_spec((1, MLA_Q_LORA)),
        _const_spec((MLA_Q_LORA, MLA_HEADS * MLA_NOPE_DIM)),
        _const_spec((MLA_Q_LORA, MLA_HEADS * LANES)),
        _const_spec((MLA_HEADS, MLA_NOPE_DIM, MLA_KV_LORA)),
        _const_spec((1, MLA_KV_LORA)),
        row_tab, row_tab, row_tab, row_tab,
        pl.BlockSpec((HEAD_DIM // 2, tm), lambda b, i: (0, i)),
        pl.BlockSpec((HEAD_DIM // 2, tm), lambda b, i: (0, i)),
        pl.BlockSpec((DIFF_QK_DIM // 2, tm), lambda b, i: (0, i)),
        pl.BlockSpec((DIFF_QK_DIM // 2, tm), lambda b, i: (0, i)),
    ]
    args = [x, wts["g_attn"], wts["w_r"], wts["w_t"], wts["g_mla_q"], wts["w_uqn"], wts["w_uqr"],
            wts["w_uk"], wts["g_mla_kv"], *tabs]
    t_rows = (D_MKV, D_MKV, D_DQK, D_DV, MLA_ROPE_DIM)

    def stacked_specs(n):
        return ([pl.BlockSpec((n, 1, tm, MLA_KV_LORA), lambda b, i: (0, b, i, 0))]
                + [pl.BlockSpec((n, 1, r, tm), lambda b, i: (0, b, 0, i)) for r in t_rows])

    if prev is not None:
        in_specs += stacked_specs(layer)
        args += list(prev)
    n = layer + 1
    out_shape = [
        jax.ShapeDtypeStruct((B, S, D_MQ), F32),
        jax.ShapeDtypeStruct((B, S, D_DQK), BF16),
        jax.ShapeDtypeStruct((B, S, D_QM), BF16),
        jax.ShapeDtypeStruct((B, S, 2 * LANES), BF16),
        jax.ShapeDtypeStruct((n, B, S, MLA_KV_LORA), F32),
    ] + [jax.ShapeDtypeStruct((n, B, r, S), F32) for r in t_rows]
    out_specs = [
        pl.BlockSpec((1, tm, D_MQ), lambda b, i: (b, i, 0)),
        pl.BlockSpec((1, tm, D_DQK), lambda b, i: (b, i, 0)),
        pl.BlockSpec((1, tm, D_QM), lambda b, i: (b, i, 0)),
        pl.BlockSpec((1, tm, 2 * LANES), lambda b, i: (b, i, 0)),
    ] + stacked_specs(n)
    outs = pl.pallas_call(
        functools.partial(_proj_kernel, layer=layer),
        grid=grid, in_specs=in_specs, out_specs=out_specs, out_shape=out_shape,
        compiler_params=pltpu.CompilerParams(
            dimension_semantics=("arbitrary", "arbitrary"), vmem_limit_bytes=VMEM_LIMIT),
        name="projection",
    )(*args)
    return outs[:4], list(outs[4:])


def _causal_tile_mask(shape):
    return lax.broadcasted_iota(jnp.int32, shape, 1) <= lax.broadcasted_iota(jnp.int32, shape, 0)


def _attn_params():
    return pltpu.CompilerParams(
        dimension_semantics=("arbitrary", "arbitrary"), vmem_limit_bytes=VMEM_LIMIT)


def _flash_blocks(j, chains):
    state = []
    for own_scores, _, pv in chains:
        s = own_scores()
        m = jnp.max(s, axis=1, keepdims=True)
        p = jnp.exp(s - m)
        state.append((m, jnp.sum(p, axis=1, keepdims=True), pv(j, p.astype(BF16))))

    def body(n, carry):
        new = []
        for (m, l, acc), (_, past_scores, pv) in zip(carry, chains):
            s = past_scores(n)
            m_new = jnp.maximum(m, jnp.max(s, axis=1, keepdims=True))
            alpha = jnp.exp(m - m_new)
            p = jnp.exp(s - m_new)
            new.append((m_new, alpha * l + jnp.sum(p, axis=1, keepdims=True),
                        alpha * acc + pv(n, p.astype(BF16))))
        return tuple(new)

    state = lax.fori_loop(0, j, body, tuple(state))
    return [acc / l for _, l, acc in state]


def _moba_prompt_kernel(mq_ref, kT_ref, vT_ref, o_ref, kaug_ref, vdup_ref, kmean_ref, *, nb):
    j = pl.program_id(1)
    blk = MOBA_BLOCK
    lane = lax.broadcasted_iota(jnp.int32, (blk, LANES), 1)

    @pl.when(j == 0)
    def _():
        kt = kT_ref[0, 0]
        vt = vT_ref[0, 0]
        row = lax.broadcasted_iota(jnp.int32, (LANES, blk), 0)
        lane_k = lax.broadcasted_iota(jnp.int32, (D_MKV, LANES), 1)
        kmean = jnp.zeros((D_MKV, LANES), F32)
        for n in range(nb):
            kb = kt[:, n * blk:(n + 1) * blk]
            vb = vt[:, n * blk:(n + 1) * blk]
            ind = jnp.where(row == n, 1.0, 0.0).astype(BF16)
            for g in range(MOBA_KV_HEADS):
                kg = kb[g * HEAD_DIM:(g + 1) * HEAD_DIM].astype(BF16)
                vg = vb[g * HEAD_DIM:(g + 1) * HEAD_DIM].astype(BF16)
                kaug_ref[g * nb + n] = jnp.concatenate([kg, kg, ind], axis=0)
                vdup_ref[g * nb + n] = jnp.concatenate([vg, vg], axis=0)
            kmean = jnp.where(lane_k == n, jnp.mean(kb, axis=1, keepdims=True), kmean)
        kmean_ref[...] = kmean

    q = mq_ref[0]
    kmean = kmean_ref[...]
    lane_f = lane.astype(F32)
    valid = lane < j
    n_sel = min(MOBA_TOPK, nb - 1)
    scale = HEAD_DIM ** -0.5
    causal = _causal_tile_mask((blk, blk))
    chains = []
    for c in range(D_MQ // LANES):
        qc = q[:, c * LANES:(c + 1) * LANES]
        for hh in range(2):
            g = (2 * c + hh) // MOBA_GROUP
            in_half = (lane < HEAD_DIM) if hh == 0 else (lane >= HEAD_DIM)
            qh = jnp.where(in_half, qc, 0.0)
            kmg = kmean[g * HEAD_DIM:(g + 1) * HEAD_DIM]
            gate = jnp.dot(qh, jnp.concatenate([kmg, kmg], axis=0),
                           precision=lax.Precision.HIGHEST, preferred_element_type=F32)
            gate = jnp.where(valid, gate, -jnp.inf)
            bias = jnp.where(valid, NEG_INF, 0.0)
            for _ in range(n_sel):
                mx = jnp.max(gate, axis=1, keepdims=True)
                first = jnp.min(jnp.where(gate == mx, lane_f, float(LANES)), axis=1, keepdims=True)
                pick = lane_f == first
                bias = jnp.where(pick, 0.0, bias)
                gate = jnp.where(pick, -jnp.inf, gate)
            qa = jnp.concatenate([(qh * scale).astype(BF16), bias.astype(BF16)], axis=1)

            def own_scores(qa=qa, g=g):
                return jnp.where(causal, _mm(qa, kaug_ref[g * nb + j]), NEG_INF)

            def past_scores(n, qa=qa, g=g):
                return _mm(qa, kaug_ref[g * nb + n])

            def pv(n, p, g=g):
                return _nt(p, vdup_ref[g * nb + n])

            chains.append((own_scores, past_scores, pv))
    outs = _flash_blocks(j, chains)
    for c in range(D_MQ // LANES):
        o_ref[0, :, c * LANES:(c + 1) * LANES] = jnp.where(
            lane < HEAD_DIM, outs[2 * c], outs[2 * c + 1]).astype(BF16)


def _moba_prompt(mq, mkT, mvT, layer):
    B, S, _ = mq.shape
    assert S % MOBA_BLOCK == 0
    nb = S // MOBA_BLOCK
    assert nb <= LANES
    kv_spec = pl.BlockSpec((1, 1, D_MKV, S), lambda b, j: (layer, b, 0, 0))
    return pl.pallas_call(
        functools.partial(_moba_prompt_kernel, nb=nb),
        grid=(B, nb),
        in_specs=[pl.BlockSpec((1, MOBA_BLOCK, D_MQ), lambda b, j: (b, j, 0)), kv_spec, kv_spec],
        out_specs=pl.BlockSpec((1, MOBA_BLOCK, D_MQ), lambda b, j: (b, j, 0)),
        out_shape=jax.ShapeDtypeStruct((B, S, D_MQ), BF16),
        scratch_shapes=[
            pltpu.VMEM((MOBA_KV_HEADS * nb, 2 * LANES, MOBA_BLOCK), BF16),
            pltpu.VMEM((MOBA_KV_HEADS * nb, LANES, MOBA_BLOCK), BF16),
            pltpu.VMEM((D_MKV, LANES), F32),
        ],
        compiler_params=_attn_params(),
        name="moba_prompt",
    )(mq, mkT, mvT)


def _diff_lambda(lq1_ref, lk1_ref, lq2_ref, lk2_ref, lam_init):
    return (jnp.exp(jnp.sum(lq1_ref[...] * lk1_ref[...], axis=1, keepdims=True))
            - jnp.exp(jnp.sum(lq2_ref[...] * lk2_ref[...], axis=1, keepdims=True)) + lam_init)


def _diff_prompt_kernel(dq_ref, kT_ref, vT_ref, lq1_ref, lk1_ref, lq2_ref, lk2_ref, gsub_ref,
                        o_ref, kbf_ref, vbf_ref, *, nk, lam_init):
    j = pl.program_id(1)
    blk = ATTN_BLK

    @pl.when(j == 0)
    def _():
        for n in range(nk):
            kbf_ref[n] = kT_ref[0, 0, :, n * blk:(n + 1) * blk].astype(BF16)
            vbf_ref[n] = vT_ref[0, 0, :, n * blk:(n + 1) * blk].astype(BF16)

    lam = _diff_lambda(lq1_ref, lk1_ref, lq2_ref, lk2_ref, lam_init)
    q = dq_ref[0]
    lane = lax.broadcasted_iota(jnp.int32, (blk, LANES), 1)
    causal = _causal_tile_mask((blk, blk))
    scale = DIFF_QK_DIM ** -0.5
    zero = jnp.zeros((), BF16)
    chains = []
    for c in range(D_DQK // LANES):
        qc = q[:, c * LANES:(c + 1) * LANES]
        rows = slice(c * LANES, (c + 1) * LANES)
        for sub in range(4):
            lo = sub * DIFF_QK_DIM
            qm = jnp.where(jnp.logical_and(lane >= lo, lane < lo + DIFF_QK_DIM), qc, zero)

            def own_scores(qm=qm, rows=rows):
                return jnp.where(causal, _mm(qm, kbf_ref[j, rows, :]) * scale, NEG_INF)

            def past_scores(n, qm=qm, rows=rows):
                return _mm(qm, kbf_ref[n, rows, :]) * scale

            def pv(n, p, rows=rows):
                return _nt(p, vbf_ref[n, rows, :])

            chains.append((own_scores, past_scores, pv))
    maps = _flash_blocks(j, chains)
    for c in range(D_DQK // LANES):
        normed = []
        for hh in range(2):
            o_h = maps[4 * c + 2 * hh] - lam * maps[4 * c + 2 * hh + 1]
            in_half = (lane < DIFF_V_DIM) if hh == 0 else (lane >= DIFF_V_DIM)
            ms = jnp.sum(jnp.where(in_half, o_h * o_h, 0.0), axis=1, keepdims=True) * (1.0 / DIFF_V_DIM)
            normed.append(o_h * lax.rsqrt(ms + SUBLN_EPS))
        out = jnp.where(lane < DIFF_V_DIM, normed[0], normed[1]) * gsub_ref[...] * (1.0 - lam_init)
        o_ref[0, :, c * LANES:(c + 1) * LANES] = out.astype(BF16)


def _diff_prompt(dq, dkT, dvT, lams, gsub2, layer, lam_init):
    B, S, _ = dq.shape
    assert S % ATTN_BLK == 0
    nk = S // ATTN_BLK
    kv_spec = pl.BlockSpec((1, 1, D_DQK, S), lambda b, j: (layer, b, 0, 0))
    lam_spec = _const_spec((1, DIFF_QK_DIM))
    return pl.pallas_call(
        functools.partial(_diff_prompt_kernel, nk=nk, lam_init=lam_init),
        grid=(B, nk),
        in_specs=[pl.BlockSpec((1, ATTN_BLK, D_DQK), lambda b, j: (b, j, 0)), kv_spec, kv_spec,
                  lam_spec, lam_spec, lam_spec, lam_spec, _const_spec((1, LANES))],
        out_specs=pl.BlockSpec((1, ATTN_BLK, D_DV), lambda b, j: (b, j, 0)),
        out_shape=jax.ShapeDtypeStruct((B, S, D_DV), BF16),
        scratch_shapes=[pltpu.VMEM((nk, D_DQK, ATTN_BLK), BF16), pltpu.VMEM((nk, D_DV, ATTN_BLK), BF16)],
        compiler_params=_attn_params(),
        name="diff_prompt",
    )(dq, dkT, dvT, *lams, gsub2)


def _mla_prompt_kernel(qm_ref, km_ref, o_ref):
    j = pl.program_id(1)
    blk = ATTN_BLK
    causal = _causal_tile_mask((blk, blk))
    scale = (MLA_NOPE_DIM + MLA_ROPE_DIM) ** -0.5
    q = qm_ref[0]

    def keys(n):
        return km_ref[0, pl.ds(pl.multiple_of(n * blk, blk), blk), :]

    def pv(n, p):
        return _mm(p, km_ref[0, pl.ds(pl.multiple_of(n * blk, blk), blk), 0:MLA_KV_LORA])

    chains = []
    for hd in range(MLA_HEADS):
        qh = q[:, hd * 2 * LANES:(hd + 1) * 2 * LANES]

        def own_scores(qh=qh):
            return jnp.where(causal, _nt(qh, keys(j)) * scale, NEG_INF)

        def past_scores(n, qh=qh):
            return _nt(qh, keys(n)) * scale

        chains.append((own_scores, past_scores, pv))
    for hd, o in enumerate(_flash_blocks(j, chains)):
        o_ref[0, :, hd * MLA_KV_LORA:(hd + 1) * MLA_KV_LORA] = o.astype(BF16)


def _mla_prompt(qm, km):
    B, S, _ = qm.shape
    assert S % ATTN_BLK == 0
    return pl.pallas_call(
        _mla_prompt_kernel,
        grid=(B, S // ATTN_BLK),
        in_specs=[pl.BlockSpec((1, ATTN_BLK, D_QM), lambda b, j: (b, j, 0)),
                  pl.BlockSpec((1, S, 2 * LANES), lambda b, j: (b, 0, 0))],
        out_specs=pl.BlockSpec((1, ATTN_BLK, D_LAT), lambda b, j: (b, j, 0)),
        out_shape=jax.ShapeDtypeStruct((B, S, D_LAT), BF16),
        compiler_params=_attn_params(),
        name="mla_prompt",
    )(qm, km)


def _mix_mlp_kernel(x_ref, om_ref, od_ref, ol_ref, wuv_ref, wo_ref, gm_ref, wup_ref, wdn_ref, gf_ref,
                    o_ref, *, final):
    attn = _mm(om_ref[...], wo_ref[0:D_MQ, :]) + _mm(od_ref[...], wo_ref[D_MQ:D_MQ + D_DV, :])
    ol = ol_ref[...]
    for c in range(MLA_HEADS // 2):
        o_mla = (_mm(ol[:, 2 * c * LANES:(2 * c + 1) * LANES], wuv_ref[2 * c])
                 + _mm(ol[:, (2 * c + 1) * LANES:(2 * c + 2) * LANES], wuv_ref[2 * c + 1]))
        r0 = D_MQ + D_DV + c * LANES
        attn = attn + _mm(o_mla.astype(BF16), wo_ref[r0:r0 + LANES, :])
    x1 = x_ref[...] + attn
    h = _rms(x1, gm_ref[...], NORM_EPS).astype(BF16)
    d_ff = wup_ref.shape[1]
    down = jnp.zeros_like(x1)
    for c in range(d_ff // FF_CHUNK):
        up = jnp.maximum(_mm(h, wup_ref[:, c * FF_CHUNK:(c + 1) * FF_CHUNK]), 0.0)
        down = down + _mm((up * up).astype(BF16), wdn_ref[c * FF_CHUNK:(c + 1) * FF_CHUNK, :])
    x2 = x1 + down
    o_ref[...] = _rms(x2, gf_ref[...], NORM_EPS) if final else x2


def _mix_mlp(x, om, od, ol, wts, g_final, final):
    N, D = x.shape
    tm = min(PROJ_ROWS, N)
    assert N % tm == 0
    d_ff = wts["w_up"].shape[1]
    assert d_ff % FF_CHUNK == 0

    def resident(shape):
        nd = len(shape)
        return pl.BlockSpec(shape, lambda i: (0,) * nd, pipeline_mode=pl.Buffered(1))

    def rows(width):
        return pl.BlockSpec((tm, width), lambda i: (i, 0))

    return pl.pallas_call(
        functools.partial(_mix_mlp_kernel, final=final),
        grid=(N // tm,),
        in_specs=[rows(D), rows(D_MQ), rows(D_DV), rows(D_LAT),
                  resident((MLA_HEADS, MLA_KV_LORA, LANES)), resident((D, D)), resident((1, D)),
                  resident((D, d_ff)), resident((d_ff, D)), resident((1, D))],
        out_specs=rows(D),
        out_shape=jax.ShapeDtypeStruct((N, D), F32),
        compiler_params=pltpu.CompilerParams(
            dimension_semantics=("arbitrary",), vmem_limit_bytes=VMEM_LIMIT),
        name="mix_mlp",
    )(x, om, od, ol, wts["w_uv"], wts["w_out"], wts["g_mlp"], wts["w_up"], wts["w_down"], g_final)


def _page_window(buf, slot, p, pos_major):
    t, r = divmod(p, PAGES_PER_TILE)
    if pos_major:
        return buf.at[slot, t, pl.ds(r * LANES, LANES), :]
    return buf.at[slot, t, :, pl.ds(r * LANES, LANES)]


def _page_copies(pt_ref, seq, slot, pools, bufs, pos_major, sems, layer, n_pages):
    copies = []
    for k, (pool, buf, pm) in enumerate(zip(pools, bufs, pos_major)):
        for p in range(n_pages):
            copies.append(pltpu.make_async_copy(
                pool.at[layer, pt_ref[seq, p]], _page_window(buf, slot, p, pm), sems.at[k, slot]))
    return copies


def _fetch_pages(pt_ref, pools, bufs, pos_major, news, sems, layer, n_pages):
    b = pl.program_id(0)
    slot = lax.rem(b, 2)

    @pl.when(b == 0)
    def _():
        for cp in _page_copies(pt_ref, 0, 0, pools, bufs, pos_major, sems, layer, n_pages):
            cp.start()

    @pl.when(b + 1 < pl.num_programs(0))
    def _():
        for cp in _page_copies(pt_ref, b + 1, 1 - slot, pools, bufs, pos_major, sems, layer, n_pages):
            cp.start()

    for cp in _page_copies(pt_ref, b, slot, pools, bufs, pos_major, sems, layer, n_pages):
        cp.wait()
    last = n_pages // PAGES_PER_TILE
    for buf, new_ref, pm in zip(bufs, news, pos_major):
        new = new_ref[0]
        if pm:
            buf[slot, last, 0:LANES, :] = new
            buf[slot, last, LANES:KEY_TILE, :] = jnp.zeros((KEY_TILE - LANES, new.shape[1]), F32)
        else:
            buf[slot, last, :, 0:LANES] = new
            buf[slot, last, :, LANES:KEY_TILE] = jnp.zeros((new.shape[0], KEY_TILE - LANES), F32)
    return slot


def _mask_new_tokens(s_ref, last, t_new):
    assert t_new & (t_new - 1) == 0
    n_rows = s_ref.shape[1]
    row = lax.broadcasted_iota(jnp.int32, (n_rows, KEY_TILE), 0)
    lane = lax.broadcasted_iota(jnp.int32, (n_rows, KEY_TILE), 1)
    s_ref[last] = jnp.where(lane <= (row & (t_new - 1)), s_ref[last], NEG_INF)


def _softmax_tiles(s_ref):
    s = s_ref[...]
    m = jnp.max(jnp.max(s, axis=0), axis=1, keepdims=True)
    p = jnp.exp(s - m[None])
    l = jnp.sum(jnp.sum(p, axis=0), axis=1, keepdims=True)
    s_ref[...] = p / l[None]


def _tile_loop(n, body, init):
    return lax.fori_loop(0, n, body, init, unroll=TILE_UNROLL)


def _moba_sample_kernel(pt_ref, q_ref, kn_ref, vn_ref, kpool, vpool, o_ref, kbuf, vbuf, sems, s_ref,
                        *, layer, n_pages, t_new):
    slot = _fetch_pages(pt_ref, (kpool, vpool), (kbuf, vbuf), (False, False), (kn_ref, vn_ref),
                        sems, layer, n_pages)
    n_blocks = n_pages // PAGES_PER_TILE
    q = q_ref[0]
    n_rows = q.shape[0]
    qb = (q * HEAD_DIM ** -0.5).astype(BF16)

    lane_k = lax.broadcasted_iota(jnp.int32, (D_MKV, LANES), 1)

    def scores(t, kmean):
        tile = kbuf[slot, t]
        s_ref[t] = _mm(qb, tile.astype(BF16))
        mean = jnp.sum(tile, axis=1, keepdims=True) * (1.0 / MOBA_BLOCK)
        return jnp.where(lane_k == t, mean, kmean)

    kmean = _tile_loop(n_blocks + 1, scores, jnp.zeros((D_MKV, LANES), F32))
    gate = jnp.dot(q, kmean, precision=lax.Precision.HIGHEST, preferred_element_type=F32)
    lane = lax.broadcasted_iota(jnp.int32, (n_rows, LANES), 1)
    lane_f = lane.astype(F32)
    valid = lane < n_blocks
    gate = jnp.where(valid, gate, -jnp.inf)
    sel = jnp.zeros((n_rows, LANES), F32)
    for _ in range(min(MOBA_TOPK, n_blocks)):
        mx = jnp.max(gate, axis=1, keepdims=True)
        first = jnp.min(jnp.where(gate == mx, lane_f, float(LANES)), axis=1, keepdims=True)
        pick = lane_f == first
        sel = jnp.where(jnp.logical_and(pick, valid), 1.0, sel)
        gate = jnp.where(pick, -jnp.inf, gate)

    def mask_block(n, _):
        chosen = jnp.sum(jnp.where(lane == n, sel, 0.0), axis=1, keepdims=True) > 0.5
        s_ref[n] = jnp.where(chosen, s_ref[n], NEG_INF)
        return 0

    lax.fori_loop(0, n_blocks, mask_block, 0, unroll=True)
    _mask_new_tokens(s_ref, n_blocks, t_new)
    _softmax_tiles(s_ref)

    def pv(t, acc):
        return acc + _nt(s_ref[t].astype(BF16), vbuf[slot, t].astype(BF16))

    o_ref[0] = _tile_loop(n_blocks + 1, pv, jnp.zeros((n_rows, D_MKV), F32))


def _diff_sample_kernel(pt_ref, q_ref, kn_ref, vn_ref, lq1_ref, lk1_ref, lq2_ref, lk2_ref, gsub_ref,
                        kpool, vpool, o_ref, kbuf, vbuf, sems, s_ref,
                        *, layer, n_pages, t_new, lam_init):
    slot = _fetch_pages(pt_ref, (kpool, vpool), (kbuf, vbuf), (False, False), (kn_ref, vn_ref),
                        sems, layer, n_pages)
    n_tiles = n_pages // PAGES_PER_TILE + 1
    q = q_ref[0]
    n_rows = q.shape[0]
    scale = DIFF_QK_DIM ** -0.5

    def scores(t, _):
        s_ref[t] = _mm(q, kbuf[slot, t].astype(BF16)) * scale
        return 0

    _tile_loop(n_tiles, scores, 0)
    _mask_new_tokens(s_ref, n_tiles - 1, t_new)
    _softmax_tiles(s_ref)
    lam = _diff_lambda(lq1_ref, lk1_ref, lq2_ref, lk2_ref, lam_init)
    half = n_rows // 2

    def pv(t, acc):
        a = (s_ref[t, 0:half, :] - lam * s_ref[t, half:n_rows, :]).astype(BF16)
        return acc + _nt(a, vbuf[slot, t].astype(BF16))

    o = _tile_loop(n_tiles, pv, jnp.zeros((half, D_DV), F32))
    assert t_new & (t_new - 1) == 0
    row = lax.broadcasted_iota(jnp.int32, o.shape, 0)
    lane = lax.broadcasted_iota(jnp.int32, o.shape, 1)
    own = (row >> (t_new.bit_length() - 1)) == (lane >> (DIFF_V_DIM.bit_length() - 1))
    ms = jnp.sum(jnp.where(own, o * o, 0.0), axis=1, keepdims=True) * (1.0 / DIFF_V_DIM)
    o_ref[0] = jnp.where(own, o * lax.rsqrt(ms + SUBLN_EPS), 0.0) * gsub_ref[...] * (1.0 - lam_init)


def _mla_sample_kernel(pt_ref, ql_ref, qp_ref, cn_ref, pn_ref, cpool, ppool, o_ref, cbuf, pbuf, sems, s_ref,
                       *, layer, n_pages, t_new):
    slot = _fetch_pages(pt_ref, (cpool, ppool), (cbuf, pbuf), (True, False), (cn_ref, pn_ref),
                        sems, layer, n_pages)
    n_tiles = n_pages // PAGES_PER_TILE + 1
    ql, qp = ql_ref[0], qp_ref[0]
    n_rows = ql.shape[0]
    scale = (MLA_NOPE_DIM + MLA_ROPE_DIM) ** -0.5

    def scores(t, _):
        s_ref[t] = (_nt(ql, cbuf[slot, t].astype(BF16)) + _mm(qp, pbuf[slot, t].astype(BF16))) * scale
        return 0

    _tile_loop(n_tiles, scores, 0)
    _mask_new_tokens(s_ref, n_tiles - 1, t_new)
    _softmax_tiles(s_ref)

    def pv(t, acc):
        return acc + _mm(s_ref[t].astype(BF16), cbuf[slot, t].astype(BF16))

    o_ref[0] = _tile_loop(n_tiles, pv, jnp.zeros((n_rows, MLA_KV_LORA), F32))


def _paged_call(kernel, page_table, per_seq, consts, pools, n_rows, out_rows, out_lanes, tile_shapes, name):
    Bd, n_pages = page_table.shape
    assert n_pages % PAGES_PER_TILE == 0
    n_tiles = n_pages // PAGES_PER_TILE + 1
    in_specs = [pl.BlockSpec((1,) + a.shape[1:], lambda b, pt: (b, 0, 0)) for a in per_seq]
    in_specs += [pl.BlockSpec(a.shape, lambda b, pt: (0, 0)) for a in consts]
    in_specs += [pl.BlockSpec(memory_space=pl.ANY) for _ in pools]
    scratch = [pltpu.VMEM((2, n_tiles) + ts, F32) for ts in tile_shapes]
    scratch.append(pltpu.SemaphoreType.DMA((len(pools), 2)))
    scratch.append(pltpu.VMEM((n_tiles, n_rows, KEY_TILE), F32))
    return pl.pallas_call(
        kernel,
        grid_spec=pltpu.PrefetchScalarGridSpec(
            num_scalar_prefetch=1, grid=(Bd,), in_specs=in_specs,
            out_specs=pl.BlockSpec((1, out_rows, out_lanes), lambda b, pt: (b, 0, 0)),
            scratch_shapes=scratch),
        out_shape=jax.ShapeDtypeStruct((Bd, out_rows, out_lanes), F32),
        compiler_params=pltpu.CompilerParams(
            dimension_semantics=("arbitrary",), vmem_limit_bytes=VMEM_LIMIT),
        name=name,
    )(page_table, *per_seq, *consts, *pools)


def _pad_lanes(a):
    return jnp.pad(a, ((0, 0), (0, 0), (0, LANES - a.shape[2])))


def _sample_attention(acts, new_rows, layer, Bd, T, page_table, pools, lams, gsub, lam_init):
    mq, dq, qm, _ = acts
    ckv_n, mkT, mvT, dkT, dvT, kpeT = new_rows
    n_pages = page_table.shape[1]
    assert MOBA_BLOCK == KEY_TILE
    kw = dict(layer=layer, n_pages=n_pages, t_new=T)

    def per_seq_cols(a):
        f = a.shape[1]
        return _pad_lanes(a[0].reshape(f, Bd, T).transpose(1, 0, 2))

    q = mq[0].reshape(Bd, T, MOBA_HEADS, HEAD_DIM).transpose(0, 2, 1, 3)
    z = jnp.zeros_like(q[:, :MOBA_GROUP])
    q_moba = jnp.concatenate([jnp.concatenate([q[:, :MOBA_GROUP], z], axis=-1),
                              jnp.concatenate([z, q[:, MOBA_GROUP:]], axis=-1)], axis=1)
    q_moba = q_moba.reshape(Bd, MOBA_HEADS * T, LANES)
    o = _paged_call(
        functools.partial(_moba_sample_kernel, **kw), page_table,
        [q_moba, per_seq_cols(mkT[layer]), per_seq_cols(mvT[layer])], [], pools[0:2],
        MOBA_HEADS * T, MOBA_HEADS * T, LANES, [(D_MKV, KEY_TILE), (D_MKV, KEY_TILE)], "moba_sample")
    o = o.reshape(Bd, MOBA_HEADS, T, MOBA_KV_HEADS, HEAD_DIM)
    om = jnp.stack([o[:, h, :, h // MOBA_GROUP] for h in range(MOBA_HEADS)], axis=2)
    om = om.reshape(Bd * T, D_MQ).astype(BF16)

    q = dq[0].reshape(Bd, T, DIFF_HEADS, 2, DIFF_QK_DIM)
    q_diff = jnp.zeros((Bd, 2, DIFF_HEADS, T, D_DQK), BF16)
    for h in range(DIFF_HEADS):
        for c in range(2):
            lo = (2 * h + c) * DIFF_QK_DIM
            q_diff = q_diff.at[:, c, h, :, lo:lo + DIFF_QK_DIM].set(q[:, :, h, c])
    q_diff = q_diff.reshape(Bd, 2 * DIFF_HEADS * T, D_DQK)
    o = _paged_call(
        functools.partial(_diff_sample_kernel, lam_init=lam_init, **kw), page_table,
        [q_diff, per_seq_cols(dkT[layer]), per_seq_cols(dvT[layer])], [*lams, gsub], pools[2:4],
        2 * DIFF_HEADS * T, DIFF_HEADS * T, D_DV, [(D_DQK, KEY_TILE), (D_DV, KEY_TILE)], "diff_sample")
    o = o.reshape(Bd, DIFF_HEADS, T, DIFF_HEADS, DIFF_V_DIM)
    od = jnp.stack([o[:, h, :, h] for h in range(DIFF_HEADS)], axis=2).reshape(Bd * T, D_DV).astype(BF16)

    q = qm[0].reshape(Bd, T, MLA_HEADS, 2 * LANES).transpose(0, 2, 1, 3).reshape(Bd, MLA_HEADS * T, 2 * LANES)
    ckv_new = jnp.pad(ckv_n[layer, 0].reshape(Bd, T, MLA_KV_LORA), ((0, 0), (0, LANES - T), (0, 0)))
    o = _paged_call(
        functools.partial(_mla_sample_kernel, **kw), page_table,
        [q[:, :, :MLA_KV_LORA], q[:, :, LANES:LANES + MLA_ROPE_DIM], ckv_new, per_seq_cols(kpeT[layer])],
        [], pools[4:6], MLA_HEADS * T, MLA_HEADS * T, MLA_KV_LORA,
        [(KEY_TILE, MLA_KV_LORA), (MLA_ROPE_DIM, KEY_TILE)], "mla_sample")
    ol = o.reshape(Bd, MLA_HEADS, T, MLA_KV_LORA).transpose(0, 2, 1, 3).reshape(Bd * T, D_LAT).astype(BF16)
    return om, od, ol


def _rope_tables(pos):
    out_rows, out_cols = [], []
    for d in (HEAD_DIM, DIFF_QK_DIM):
        half = d // 2
        inv = 1.0 / (ROPE_THETA ** (jnp.arange(half, dtype=F32) * (2.0 / d)))
        ang = pos.astype(F32)[:, None] * inv[None, :]
        cos, sin = jnp.cos(ang), jnp.sin(ang)
        out_rows += [jnp.tile(cos, (1, LANES // half)),
                     jnp.tile(jnp.concatenate([-sin, sin], axis=1), (1, LANES // d))]
        out_cols += [cos.T, sin.T]
    return out_rows + out_cols


def _layer_weights(l, g_attn, w_in, g_mla_q, w_mla_uq, g_mla_kv, w_mla_uk, w_mla_uv, w_out, g_mlp,
                   w_up, w_down):
    w = w_in[l]
    d = w.shape[0]
    w_r = jnp.concatenate([w[:, _O_MQ:_O_MK], w[:, _O_DQ:_O_DK], w[:, _O_CQ:_O_KPE], w[:, _O_KPE:_O_END],
                           jnp.zeros((d, LANES - MLA_ROPE_DIM), w.dtype)], axis=1)
    w_t = jnp.concatenate([w[:, _O_MK:_O_DQ], w[:, _O_DK:_O_CQ], w[:, _O_KPE:_O_END]], axis=1).T
    uq = w_mla_uq[l].reshape(MLA_Q_LORA, MLA_HEADS, MLA_NOPE_DIM + MLA_ROPE_DIM)
    w_uqn = uq[:, :, :MLA_NOPE_DIM].reshape(MLA_Q_LORA, MLA_HEADS * MLA_NOPE_DIM)
    w_uqr = jnp.pad(uq[:, :, MLA_NOPE_DIM:], ((0, 0), (0, 0), (0, LANES - MLA_ROPE_DIM)))
    w_uqr = w_uqr.reshape(MLA_Q_LORA, MLA_HEADS * LANES)
    w_uk = w_mla_uk[l].transpose(1, 2, 0)
    uv = w_mla_uv[l].transpose(1, 0, 2)
    w_uv = jnp.stack([jnp.pad(uv[h], ((0, 0), ((h % 2) * MLA_V_DIM, LANES - MLA_V_DIM - (h % 2) * MLA_V_DIM)))
                      for h in range(MLA_HEADS)], axis=0)
    return {
        "g_attn": g_attn[l][None], "w_r": w_r.astype(BF16), "w_t": w_t.astype(BF16),
        "g_mla_q": g_mla_q[l][None], "w_uqn": w_uqn.astype(BF16), "w_uqr": w_uqr.astype(BF16),
        "w_uk": w_uk.astype(BF16), "g_mla_kv": g_mla_kv[l][None], "w_uv": w_uv.astype(BF16),
        "w_out": w_out[l].astype(BF16), "g_mlp": g_mlp[l][None],
        "w_up": w_up[l].astype(BF16), "w_down": w_down[l].astype(BF16),
    }


def _feature_major_pool(cache):
    nd = cache.ndim
    perm = (0, 1) + tuple(range(3, nd)) + (2,)
    t = cache.transpose(perm)
    return t.reshape(t.shape[0], t.shape[1], -1, t.shape[-1])


def kernel(x_prompt, x_sample, cache_moba_k, cache_moba_v, cache_diff_k, cache_diff_v, cache_mla_ckv, cache_mla_kpe, page_table, g_attn, w_in, g_mla_q, w_mla_uq, g_mla_kv, w_mla_uk, w_mla_uv, lambda_q1, lambda_k1, lambda_q2, lambda_k2, g_diff_sub, w_out, g_mlp, w_up, w_down, g_final):
    B, S, D = x_prompt.shape
    Bd, T, _ = x_sample.shape
    depth = w_in.shape[0]
    n_pages = page_table.shape[1]
    past_len = n_pages * cache_moba_k.shape[2]
    assert cache_moba_k.shape[2] == LANES
    assert past_len % MOBA_BLOCK == 0

    tabs_p = _rope_tables(jnp.arange(S))
    tabs_s = _rope_tables(past_len + (jnp.arange(Bd * T) % T))
    pools = [_feature_major_pool(cache_moba_k), _feature_major_pool(cache_moba_v),
             _feature_major_pool(cache_diff_k), _feature_major_pool(cache_diff_v),
             cache_mla_ckv, _feature_major_pool(cache_mla_kpe)]
    gf = g_final[None]

    xp = x_prompt
    xs = x_sample.reshape(1, Bd * T, D)
    rows_p = rows_s = None
    for l in range(depth):
        final = l == depth - 1
        lam_init = 0.8 - 0.6 * math.exp(-0.3 * l)
        wts = _layer_weights(l, g_attn, w_in, g_mla_q, w_mla_uq, g_mla_kv, w_mla_uk, w_mla_uv, w_out,
                             g_mlp, w_up, w_down)
        lams = [lambda_q1[l][None], lambda_k1[l][None], lambda_q2[l][None], lambda_k2[l][None]]
        gsub = g_diff_sub[l]
        gsub2 = jnp.tile(gsub, LANES // DIFF_V_DIM)[None]
        gsub4 = jnp.tile(gsub, DIFF_HEADS)[None]

        (mq, dq, qm, km), rows_p = _projection(xp, l, wts, tabs_p, rows_p)
        om = _moba_prompt(mq, rows_p[1], rows_p[2], l)
        od = _diff_prompt(dq, rows_p[3], rows_p[4], lams, gsub2, l, lam_init)
        ol = _mla_prompt(qm, km)
        xp = _mix_mlp(xp.reshape(B * S, D), om.reshape(B * S, D_MQ), od.reshape(B * S, D_DV),
                      ol.reshape(B * S, D_LAT), wts, gf, final).reshape(B, S, D)

        acts, rows_s = _projection(xs, l, wts, tabs_s, rows_s)
        om, od, ol = _sample_attention(acts, rows_s, l, Bd, T, page_table, pools, lams, gsub4, lam_init)
        xs = _mix_mlp(xs.reshape(Bd * T, D), om, od, ol, wts, gf, final).reshape(1, Bd * T, D)

    def finish(rows, b, s):
        ckv, mkT, mvT, dkT, dvT, kpeT = rows
        mk = mkT.reshape(depth, b, MOBA_KV_HEADS, HEAD_DIM, s).transpose(0, 1, 4, 2, 3)
        mv = mvT.reshape(depth, b, MOBA_KV_HEADS, HEAD_DIM, s).transpose(0, 1, 4, 2, 3)
        dk = dkT.reshape(depth, b, DIFF_HEADS, 2, DIFF_QK_DIM, s).transpose(0, 1, 5, 2, 3, 4)
        dv = dvT.reshape(depth, b, DIFF_HEADS, DIFF_V_DIM, s).transpose(0, 1, 4, 2, 3)
        return mk, mv, dk, dv, ckv, kpeT.transpose(0, 1, 3, 2)

    outs_p = finish(rows_p, B, S)
    outs_s = tuple(a.reshape((depth, Bd, T) + a.shape[3:]) for a in finish(rows_s, 1, Bd * T))
    return (xp, xs.reshape(Bd, T, D)) + outs_p + outs_s
```
